```python
import math
import jax, jax.numpy as jnp
from jax import lax
import numpy as np

D_MODEL = 1024
BATCH = 8
SEQ = 8192
DEPTH = 1

N_HEADS = 8
HEAD_DIM = 64
ATTN_WIDTH = N_HEADS * HEAD_DIM
MOBA_BLOCK = 256
MOBA_TOPK = 3
Q_CHUNK = 64
ROPE_THETA = 500000.0
ROT_DIM = HEAD_DIM // 4
CONV_GROUPS = 8
CONV_WIDTH = 512
CONV_K = 3
D_FF = 2816
FFN_CONV_K = 3
GATE_WIDTH = 2 * D_MODEL
IN_WIDTH = 3 * ATTN_WIDTH + 3 * CONV_WIDTH + GATE_WIDTH
DEEPNORM_ALPHA = (2.0 * DEPTH) ** 0.25
DEEPNORM_BETA = (8.0 * DEPTH) ** -0.25
LN_EPS = 1e-5

kernel_name = "hybrid_moba_shortconv_convffn_deepnorm"


def layer_norm(x, g, b):
    xf = x.astype(jnp.float32)
    mu = jnp.mean(xf, axis=-1, keepdims=True)
    var = jnp.mean(jnp.square(xf - mu), axis=-1, keepdims=True)
    y = (xf - mu) * lax.rsqrt(var + LN_EPS) * g.astype(jnp.float32) + b.astype(jnp.float32)
    return y.astype(x.dtype)


def causal_dwconv(x, w):
    k = w.shape[0]
    return lax.conv_general_dilated(
        x, w[:, None, :].astype(x.dtype), window_strides=(1,), padding=[(k - 1, 0)],
        dimension_numbers=("NWC", "WIO", "NWC"), feature_group_count=x.shape[-1])


def rope_tables(positions, dtype):
    inv_freq = ROPE_THETA ** (-jnp.arange(0, ROT_DIM, 2, dtype=jnp.float32) / ROT_DIM)
    ang = positions.astype(jnp.float32)[..., None] * inv_freq
    return jnp.cos(ang)[:, :, None, :].astype(dtype), jnp.sin(ang)[:, :, None, :].astype(dtype)


def apply_partial_rope(x, cos, sin):
    xr, xp = x[..., :ROT_DIM], x[..., ROT_DIM:]
    x1, x2 = xr[..., :ROT_DIM // 2], xr[..., ROT_DIM // 2:]
    rot = jnp.concatenate([x1 * cos - x2 * sin, x2 * cos + x1 * sin], axis=-1)
    return jnp.concatenate([rot, xp], axis=-1)


def moba_attention(q, k, v):
    bsz, s, h, dh = q.shape
    nb = -(-s // MOBA_BLOCK)
    s_pad = nb * MOBA_BLOCK
    pad = ((0, 0), (0, s_pad - s), (0, 0), (0, 0))
    k_blk = jnp.pad(k, pad).reshape(bsz, nb, MOBA_BLOCK, h, dh).transpose(0, 3, 1, 2, 4)
    v_blk = jnp.pad(v, pad).reshape(bsz, nb, MOBA_BLOCK, h, dh).transpose(0, 3, 1, 2, 4)
    k_mean = jnp.mean(k_blk.astype(jnp.float32), axis=3)
    topk = min(MOBA_TOPK, nb)
    scale = 1.0 / math.sqrt(dh)
    nqc = s // Q_CHUNK
    q_c = q.reshape(bsz, nqc, Q_CHUNK, h, dh).transpose(0, 1, 3, 2, 4).reshape(bsz * nqc, h, Q_CHUNK, dh)
    b_idx = jnp.repeat(jnp.arange(bsz, dtype=jnp.int32), nqc)
    c_idx = jnp.tile(jnp.arange(nqc, dtype=jnp.int32), bsz)
    head_ix = jnp.arange(h)[:, None, None]
    blk_ids = jnp.arange(nb, dtype=jnp.int32)
    in_blk = jnp.arange(MOBA_BLOCK, dtype=jnp.int32)
    in_chunk = jnp.arange(Q_CHUNK, dtype=jnp.int32)

    def chunk_body(args):
        qc, b, c = args
        kb, vb, km = k_blk[b], v_blk[b], k_mean[b]
        q_pos = c * Q_CHUNK + in_chunk
        own = (c * Q_CHUNK) // MOBA_BLOCK
        s_blk = jnp.einsum("hqd,hnd->hqn", qc.astype(jnp.float32), km)
        s_blk = jnp.where((blk_ids < own)[None, None, :], s_blk, -jnp.inf)
        _, sel = lax.top_k(s_blk, topk)
        valid = sel < own
        k_sel = kb[head_ix, sel]
        v_sel = vb[head_ix, sel]
        s_sel = jnp.einsum("hqd,hqjtd->hqjt", qc, k_sel, preferred_element_type=jnp.float32) * scale
        s_sel = jnp.where(valid[..., None], s_sel, -jnp.inf).reshape(h, Q_CHUNK, topk * MOBA_BLOCK)
        k_own = lax.dynamic_index_in_dim(kb, own, axis=1, keepdims=False)
        v_own = lax.dynamic_index_in_dim(vb, own, axis=1, keepdims=False)
        s_own = jnp.einsum("hqd,htd->hqt", qc, k_own, preferred_element_type=jnp.float32) * scale
        key_pos = own * MOBA_BLOCK + in_blk
        s_own = jnp.where(key_pos[None, None, :] <= q_pos[None, :, None], s_own, -jnp.inf)
        p = jax.nn.softmax(jnp.concatenate([s_sel, s_own], axis=-1), axis=-1).astype(v.dtype)
        p_sel = p[..., :topk * MOBA_BLOCK].reshape(h, Q_CHUNK, topk, MOBA_BLOCK)
        p_own = p[..., topk * MOBA_BLOCK:]
        return (jnp.einsum("hqjt,hqjtd->hqd", p_sel, v_sel)
                + jnp.einsum("hqt,htd->hqd", p_own, v_own))

    o = lax.map(chunk_body, (q_c, b_idx, c_idx))
    return o.reshape(bsz, nqc, h, Q_CHUNK, dh).transpose(0, 1, 3, 2, 4).reshape(bsz, s, h * dh)


def setup_inputs(seed: int = 0) -> dict:
    key = jax.random.key(seed)
    ks = jax.random.split(key, 16)
    f32 = jnp.float32
    nrm = lambda k, shape, s: jax.random.normal(k, shape, f32) * s
    return {
        "x": jax.random.normal(ks[0], (BATCH, SEQ, D_MODEL), f32),
        "positions": jnp.broadcast_to(jnp.arange(SEQ, dtype=jnp.int32), (BATCH, SEQ)),
        "w_in": nrm(ks[1], (DEPTH, D_MODEL, IN_WIDTH), D_MODEL ** -0.5),
        "b_gate": nrm(ks[2], (DEPTH, GATE_WIDTH), 0.1),
        "w_attn_out": nrm(ks[3], (DEPTH, ATTN_WIDTH, D_MODEL), ATTN_WIDTH ** -0.5),
        "conv_w_mix": nrm(ks[4], (DEPTH, CONV_K, CONV_WIDTH), CONV_K ** -0.5),
        "w_conv_out": nrm(ks[5], (DEPTH, CONV_WIDTH, D_MODEL), CONV_WIDTH ** -0.5),
        "w_o": nrm(ks[6], (DEPTH, D_MODEL, D_MODEL), DEEPNORM_BETA * D_MODEL ** -0.5),
        "ln1_g": 1.0 + nrm(ks[7], (DEPTH, D_MODEL), 0.02),
        "ln1_b": nrm(ks[8], (DEPTH, D_MODEL), 0.02),
        "w_up": nrm(ks[9], (DEPTH, D_MODEL, 2 * D_FF), D_MODEL ** -0.5),
        "conv_w_ffn": nrm(ks[10], (DEPTH, FFN_CONV_K, 2 * D_FF), FFN_CONV_K ** -0.5),
        "w_down": nrm(ks[11], (DEPTH, D_FF, D_MODEL), DEEPNORM_BETA * D_FF ** -0.5),
        "ln2_g": 1.0 + nrm(ks[12], (DEPTH, D_MODEL), 0.02),
        "ln2_b": nrm(ks[13], (DEPTH, D_MODEL), 0.02),
    }


def reference(x, positions, w_in, b_gate, w_attn_out, conv_w_mix, w_conv_out, w_o,
              ln1_g, ln1_b, w_up, conv_w_ffn, w_down, ln2_g, ln2_b):
    bsz, s, _ = x.shape
    cos, sin = rope_tables(positions, x.dtype)
    a = ATTN_WIDTH
    c = CONV_WIDTH
    for l in range(DEPTH):
        proj = x @ w_in[l]
        q = proj[..., 0:a].reshape(bsz, s, N_HEADS, HEAD_DIM)
        k = proj[..., a:2 * a].reshape(bsz, s, N_HEADS, HEAD_DIM)
        v = proj[..., 2 * a:3 * a].reshape(bsz, s, N_HEADS, HEAD_DIM)
        off = 3 * a
        gate_b = proj[..., off:off + c]
        gate_c = proj[..., off + c:off + 2 * c]
        h_in = proj[..., off + 2 * c:off + 3 * c]
        gates = jax.nn.sigmoid(proj[..., off + 3 * c:] + b_gate[l])
        g_att, g_cnv = gates[..., :D_MODEL], gates[..., D_MODEL:]
        q = apply_partial_rope(q, cos, sin)
        k = apply_partial_rope(k, cos, sin)
        y_att = moba_attention(q, k, v) @ w_attn_out[l]
        y_cnv = (gate_b * causal_dwconv(gate_c * h_in, conv_w_mix[l])) @ w_conv_out[l]
        mix = (g_att * y_att + g_cnv * y_cnv) @ w_o[l]
        x = layer_norm(DEEPNORM_ALPHA * x + mix, ln1_g[l], ln1_b[l])
        u = causal_dwconv(x @ w_up[l], conv_w_ffn[l])
        f = (jax.nn.silu(u[..., :D_FF]) * u[..., D_FF:]) @ w_down[l]
        x = layer_norm(DEEPNORM_ALPHA * x + f, ln2_g[l], ln2_b[l])
    return x
```

```python
import functools
import math

import jax
import jax.numpy as jnp
from jax import lax
from jax.experimental import pallas as pl
from jax.experimental.pallas import tpu as pltpu

N_HEADS = 8
HEAD_DIM = 64
ATTN_WIDTH = N_HEADS * HEAD_DIM
MOBA_BLOCK = 256
MOBA_TOPK = 3
ROPE_THETA = 500000.0
ROT_DIM = HEAD_DIM // 4
ROT_HALF = ROT_DIM // 2
CONV_WIDTH = 512
CONV_K = 3
D_FF = 2816
DEPTH = 1
DEEPNORM_ALPHA = (2.0 * DEPTH) ** 0.25
LN_EPS = 1e-5

Q_SCALE = (1.0 / math.sqrt(HEAD_DIM)) * math.log2(math.e)

HEADS_PER_GROUP = 2
GROUP_WIDTH = HEADS_PER_GROUP * HEAD_DIM
N_GROUPS = N_HEADS // HEADS_PER_GROUP
SUBLANES = 8
TOKEN_TILE = 512
FF_CHUNK = 256
VMEM_LIMIT_BYTES = 56 * 1024 * 1024

_NT_DIMS = (((1,), (1,)), ((), ()))
_F32 = jnp.float32
_BF16 = jnp.bfloat16


def _resident(shape):
    nd = len(shape)
    return pl.BlockSpec(shape, lambda *_: (0,) * nd, pipeline_mode=pl.Buffered(1))


def _shifted_rows(buf_ref, cur, carry, tile):
    buf_ref[0:SUBLANES, :] = carry
    buf_ref[SUBLANES:SUBLANES + tile, :] = cur
    prev2 = buf_ref[SUBLANES - 2:SUBLANES - 2 + tile, :]
    prev1 = buf_ref[SUBLANES - 1:SUBLANES - 1 + tile, :]
    return prev2, prev1


def _layer_norm(z, g, b):
    mu = jnp.mean(z, axis=-1, keepdims=True)
    zc = z - mu
    var = jnp.mean(zc * zc, axis=-1, keepdims=True)
    return zc * lax.rsqrt(var + LN_EPS) * g + b


def _inproj_kernel(x_ref, pos_ref, invf_ref, wqkvt_ref, wrest_ref, bg_ref, cw_ref, wcnv_ref,
                   qt_ref, k_ref, vt_ref, gatt_ref, cnvg_ref, buf_ref, carry_ref):
    tile = x_ref.shape[1]
    d_model = x_ref.shape[2]
    xb = x_ref[0].astype(_BF16)

    qkvt = lax.dot_general(wqkvt_ref[...], xb, _NT_DIMS, preferred_element_type=_F32)
    ang = invf_ref[...] * pos_ref[0].astype(_F32)
    cos, sin = jnp.cos(ang), jnp.sin(ang)

    def rope_t(t):
        parts = []
        for h in range(N_HEADS):
            r = h * HEAD_DIM
            x1 = t[r:r + ROT_HALF]
            x2 = t[r + ROT_HALF:r + ROT_DIM]
            parts += [x1 * cos - x2 * sin, x2 * cos + x1 * sin, t[r + ROT_DIM:r + HEAD_DIM]]
        return jnp.concatenate(parts, axis=0)

    a = ATTN_WIDTH
    qt_ref[0] = (rope_t(qkvt[0:a]) * Q_SCALE).astype(_BF16)
    k_ref[0] = rope_t(qkvt[a:2 * a]).T.astype(_BF16)
    vt = qkvt[2 * a:3 * a].astype(_BF16)
    for j in range(tile // MOBA_BLOCK):
        vt_ref[0, j] = vt[:, j * MOBA_BLOCK:(j + 1) * MOBA_BLOCK]

    rest = jnp.dot(xb, wrest_ref[...], preferred_element_type=_F32)
    c = CONV_WIDTH
    gate_b, gate_c, h_in = rest[:, 0:c], rest[:, c:2 * c], rest[:, 2 * c:3 * c]
    ch = gate_c * h_in

    @pl.when(pl.program_id(1) == 0)
    def _():
        carry_ref[...] = jnp.zeros_like(carry_ref)

    prev2, prev1 = _shifted_rows(buf_ref, ch, carry_ref[...], tile)
    carry_ref[...] = ch[tile - SUBLANES:tile]
    cw = cw_ref[...]
    conv = cw[0:1] * prev2 + cw[1:2] * prev1 + cw[2:3] * ch
    y_cnv = jnp.dot((gate_b * conv).astype(_BF16), wcnv_ref[...], preferred_element_type=_F32)

    gates = jax.nn.sigmoid(rest[:, 3 * c:] + bg_ref[...])
    gatt_ref[0] = gates[:, :d_model].astype(_BF16)
    cnvg_ref[0] = (gates[:, d_model:] * y_cnv).astype(_BF16)


def _inproj(x, pos3, inv_freq, wqkvt, wrest, bg, cw, wcnv):
    bsz, s, d = x.shape
    tile = TOKEN_TILE
    nb = s // MOBA_BLOCK
    a = ATTN_WIDTH
    tok = lambda w: pl.BlockSpec((1, tile, w), lambda b, t: (b, t, 0))
    return pl.pallas_call(
        _inproj_kernel,
        grid=(bsz, s // tile),
        in_specs=[
            tok(d),
            pl.BlockSpec((1, 1, tile), lambda b, t: (b, 0, t)),
            _resident(inv_freq.shape),
            _resident(wqkvt.shape),
            _resident(wrest.shape),
            _resident(bg.shape),
            _resident(cw.shape),
            _resident(wcnv.shape),
        ],
        out_specs=[
            pl.BlockSpec((1, a, tile), lambda b, t: (b, 0, t)),
            tok(a),
            pl.BlockSpec((1, tile // MOBA_BLOCK, a, MOBA_BLOCK), lambda b, t: (b, t, 0, 0)),
            tok(d),
            tok(d),
        ],
        out_shape=[
            jax.ShapeDtypeStruct((bsz, a, s), _BF16),
            jax.ShapeDtypeStruct((bsz, s, a), _BF16),
            jax.ShapeDtypeStruct((bsz, nb, a, MOBA_BLOCK), _BF16),
            jax.ShapeDtypeStruct((bsz, s, d), _BF16),
            jax.ShapeDtypeStruct((bsz, s, d), _BF16),
        ],
        scratch_shapes=[
            pltpu.VMEM((tile + SUBLANES, CONV_WIDTH), _F32),
            pltpu.VMEM((SUBLANES, CONV_WIDTH), _F32),
        ],
        compiler_params=pltpu.CompilerParams(
            dimension_semantics=("arbitrary", "arbitrary"), vmem_limit_bytes=VMEM_LIMIT_BYTES),
        name="inproj",
    )(x, pos3, inv_freq, wqkvt, wrest, bg, cw, wcnv)


def _moba_kernel(qt_ref, k_ref, vt_ref, o_ref, kmean_ref, mrow_ref):
    i = pl.program_id(2)
    nb = vt_ref.shape[1]
    blk = MOBA_BLOCK
    neg_inf = -jnp.inf

    @pl.when(i == 0)
    def _():
        def mean_body(j, _):
            kb = k_ref[0, pl.ds(pl.multiple_of(j * blk, blk), blk), :].astype(_F32)
            kmean_ref[pl.ds(j, 1), :] = jnp.sum(kb, axis=0, keepdims=True) / blk
            return 0
        lax.fori_loop(0, nb, mean_body, 0)

    qt = qt_ref[0]
    grow = lax.broadcasted_iota(jnp.int32, qt.shape, 0)
    q_heads = [jnp.where((grow >= h * HEAD_DIM) & (grow < (h + 1) * HEAD_DIM), qt, jnp.zeros_like(qt))
               for h in range(HEADS_PER_GROUP)]

    kmean = kmean_ref[...].astype(_BF16)
    brow = lax.broadcasted_iota(jnp.int32, (nb, blk), 0)
    for h in range(HEADS_PER_GROUP):
        sc = jnp.dot(kmean, q_heads[h], preferred_element_type=_F32)
        sc = jnp.where(brow < i, sc, neg_inf)
        sel = jnp.zeros((nb, blk), dtype=jnp.bool_)
        for _ in range(MOBA_TOPK):
            best = jnp.max(sc, axis=0, keepdims=True)
            idx = jnp.min(jnp.where(sc == best, brow, nb), axis=0, keepdims=True)
            pick = brow == idx
            sel = sel | pick
            sc = jnp.where(pick, neg_inf, sc)
        sel = sel & (brow < i)
        mrow_ref[h] = jnp.where(sel, 0.0, jnp.inf).astype(_F32)

    row0 = pl.multiple_of(i * blk, blk)
    k_own = k_ref[0, pl.ds(row0, blk), :]
    vt_own = vt_ref[0, i]
    kpos = lax.broadcasted_iota(jnp.int32, (blk, blk), 0)
    qpos = lax.broadcasted_iota(jnp.int32, (blk, blk), 1)
    state = []
    for h in range(HEADS_PER_GROUP):
        s = jnp.dot(k_own, q_heads[h], preferred_element_type=_F32)
        s = jnp.where(kpos <= qpos, s, neg_inf)
        m = jnp.max(s, axis=0, keepdims=True)
        p = jnp.exp2(s - m)
        l = jnp.sum(p, axis=0, keepdims=True)
        acc = jnp.dot(vt_own[h * HEAD_DIM:(h + 1) * HEAD_DIM], p.astype(_BF16), preferred_element_type=_F32)
        state += [m, l, acc]

    def blk_body(j, carry):
        kj = k_ref[0, pl.ds(pl.multiple_of(j * blk, blk), blk), :]
        vtj = vt_ref[0, j]
        out = []
        for h in range(HEADS_PER_GROUP):
            m, l, acc = carry[3 * h:3 * h + 3]
            off = mrow_ref[h, pl.ds(j, 1), :]
            s = jnp.dot(kj, q_heads[h], preferred_element_type=_F32)
            m_blk = jnp.max(s, axis=0, keepdims=True) - off
            m_new = jnp.maximum(m, m_blk)
            alpha = jnp.exp2(m - m_new)
            p = jnp.exp2(s - (m_new + off))
            l = alpha * l + jnp.sum(p, axis=0, keepdims=True)
            acc = alpha * acc + jnp.dot(vtj[h * HEAD_DIM:(h + 1) * HEAD_DIM], p.astype(_BF16),
                                        preferred_element_type=_F32)
            out += [m_new, l, acc]
        return tuple(out)

    state = lax.fori_loop(0, i, blk_body, tuple(state))
    ot = jnp.concatenate([state[3 * h + 2] / state[3 * h + 1] for h in range(HEADS_PER_GROUP)], axis=0)
    o_ref[0] = ot.T.astype(_BF16)


def _moba(qt, k, vt):
    bsz, a, s = qt.shape
    nb = s // MOBA_BLOCK
    gw = GROUP_WIDTH
    return pl.pallas_call(
        _moba_kernel,
        grid=(bsz, N_GROUPS, nb),
        in_specs=[
            pl.BlockSpec((1, gw, MOBA_BLOCK), lambda b, g, i: (b, g, i)),
            pl.BlockSpec((1, s, gw), lambda b, g, i: (b, 0, g)),
            pl.BlockSpec((1, nb, gw, MOBA_BLOCK), lambda b, g, i: (b, 0, g, 0)),
        ],
        out_specs=pl.BlockSpec((1, MOBA_BLOCK, gw), lambda b, g, i: (b, i, g)),
        out_shape=jax.ShapeDtypeStruct((bsz, s, a), _BF16),
        scratch_shapes=[
            pltpu.VMEM((nb, gw), _F32),
            pltpu.VMEM((HEADS_PER_GROUP, nb, MOBA_BLOCK), _F32),
        ],
        compiler_params=pltpu.CompilerParams(
            dimension_semantics=("arbitrary", "arbitrary", "arbitrary"), vmem_limit_bytes=VMEM_LIMIT_BYTES),
        name="moba",
    )(qt, k, vt)


def _post_kernel(x_ref, attn_ref, gatt_ref, cnvg_ref, watt_ref, wo_ref, g1_ref, b1_ref,
                 wupg_ref, wupv_ref, cwg_ref, cwv_ref, wdown_ref, g2_ref, b2_ref,
                 o_ref, buf_ref, carryg_ref, carryv_ref, hmid_ref):
    tile = x_ref.shape[1]
    n_chunks = wupg_ref.shape[0]

    y_att = jnp.dot(attn_ref[0], watt_ref[...], preferred_element_type=_F32)
    t = gatt_ref[0].astype(_F32) * y_att + cnvg_ref[0].astype(_F32)
    mix = jnp.dot(t.astype(_BF16), wo_ref[...], preferred_element_type=_F32)
    x1 = _layer_norm(DEEPNORM_ALPHA * x_ref[0] + mix, g1_ref[...], b1_ref[...])
    x1b = x1.astype(_BF16)

    @pl.when(pl.program_id(1) == 0)
    def _():
        carryg_ref[...] = jnp.zeros_like(carryg_ref)
        carryv_ref[...] = jnp.zeros_like(carryv_ref)

    def conv(u, cw_ref, carry_ref, c):
        prev2, prev1 = _shifted_rows(buf_ref, u, carry_ref[c], tile)
        carry_ref[c] = u[tile - SUBLANES:tile]
        cw = cw_ref[c]
        return cw[0:1] * prev2 + cw[1:2] * prev1 + cw[2:3] * u

    def chunk_body(c, _):
        ug = jnp.dot(x1b, wupg_ref[c], preferred_element_type=_F32)
        uv = jnp.dot(x1b, wupv_ref[c], preferred_element_type=_F32)
        cg = conv(ug, cwg_ref, carryg_ref, c)
        cv = conv(uv, cwv_ref, carryv_ref, c)
        hmid_ref[c] = (cg * jax.nn.sigmoid(cg) * cv).astype(_BF16)
        return 0

    lax.fori_loop(0, n_chunks, chunk_body, 0)

    f = jnp.dot(hmid_ref[0], wdown_ref[0], preferred_element_type=_F32)
    for c in range(1, n_chunks):
        f = f + jnp.dot(hmid_ref[c], wdown_ref[c], preferred_element_type=_F32)
    o_ref[0] = _layer_norm(DEEPNORM_ALPHA * x1 + f, g2_ref[...], b2_ref[...])


def _post(x, attn, gatt, cnvg, watt, wo, g1, b1, wupg, wupv, cwg, cwv, wdown, g2, b2):
    bsz, s, d = x.shape
    tile = TOKEN_TILE
    n_chunks = wupg.shape[0]
    tok = lambda w: pl.BlockSpec((1, tile, w), lambda b, t: (b, t, 0))
    weights = (watt, wo, g1, b1, wupg, wupv, cwg, cwv, wdown, g2, b2)
    return pl.pallas_call(
        _post_kernel,
        grid=(bsz, s // tile),
        in_specs=[tok(d), tok(ATTN_WIDTH), tok(d), tok(d)] + [_resident(w.shape) for w in weights],
        out_specs=tok(d),
        out_shape=jax.ShapeDtypeStruct((bsz, s, d), x.dtype),
        scratch_shapes=[
            pltpu.VMEM((tile + SUBLANES, FF_CHUNK), _F32),
            pltpu.VMEM((n_chunks, SUBLANES, FF_CHUNK), _F32),
            pltpu.VMEM((n_chunks, SUBLANES, FF_CHUNK), _F32),
            pltpu.VMEM((n_chunks, tile, FF_CHUNK), _BF16),
        ],
        compiler_params=pltpu.CompilerParams(
            dimension_semantics=("arbitrary", "arbitrary"), vmem_limit_bytes=VMEM_LIMIT_BYTES),
        name="post",
    )(x, attn, gatt, cnvg, *weights)


def _chunk_cols(w, n_chunks):
    return w.reshape(w.shape[0], n_chunks, FF_CHUNK).transpose(1, 0, 2)


def kernel(x, positions, w_in, b_gate, w_attn_out, conv_w_mix, w_conv_out, w_o,
           ln1_g, ln1_b, w_up, conv_w_ffn, w_down, ln2_g, ln2_b):
    bsz, s, d = x.shape
    assert s % TOKEN_TILE == 0 and TOKEN_TILE % MOBA_BLOCK == 0 and D_FF % FF_CHUNK == 0
    a = ATTN_WIDTH
    n_chunks = D_FF // FF_CHUNK
    inv_freq = (ROPE_THETA ** (-jnp.arange(0, ROT_DIM, 2, dtype=_F32) / ROT_DIM)).reshape(ROT_HALF, 1)
    pos3 = positions.reshape(bsz, 1, s)
    for l in range(DEPTH):
        wqkvt = w_in[l, :, :3 * a].T.astype(_BF16)
        wrest = w_in[l, :, 3 * a:].astype(_BF16)
        qt, k, vt, gatt, cnvg = _inproj(
            x, pos3, inv_freq, wqkvt, wrest, b_gate[l].reshape(1, -1), conv_w_mix[l],
            w_conv_out[l].astype(_BF16))
        attn = _moba(qt, k, vt)
        row = lambda v: v.reshape(1, -1)
        x = _post(
            x, attn, gatt, cnvg, w_attn_out[l].astype(_BF16), w_o[l].astype(_BF16),
            row(ln1_g[l]), row(ln1_b[l]),
            _chunk_cols(w_up[l, :, :D_FF].astype(_BF16), n_chunks),
            _chunk_cols(w_up[l, :, D_FF:].astype(_BF16), n_chunks),
            _chunk_cols(conv_w_ffn[l, :, :D_FF], n_chunks),
            _chunk_cols(conv_w_ffn[l, :, D_FF:], n_chunks),
            w_down[l].astype(_BF16).reshape(n_chunks, FF_CHUNK, d),
            row(ln2_g[l]), row(ln2_b[l]))
    return x
```

```python
import functools
import math

import jax
import jax.numpy as jnp
from jax import lax
from jax.experimental import pallas as pl
from jax.experimental.pallas import tpu as pltpu

N_HEADS = 8
HEAD_DIM = 64
ATTN_WIDTH = N_HEADS * HEAD_DIM
MOBA_BLOCK = 256
MOBA_TOPK = 3
ROPE_THETA = 500000.0
ROT_DIM = HEAD_DIM // 4
ROT_HALF = ROT_DIM // 2
CONV_WIDTH = 512
CONV_K = 3
D_FF = 2816
DEPTH = 1
DEEPNORM_ALPHA = (2.0 * DEPTH) ** 0.25
LN_EPS = 1e-5

Q_SCALE = (1.0 / math.sqrt(HEAD_DIM)) * math.log2(math.e)

HEADS_PER_GROUP = 2
GROUP_WIDTH = HEADS_PER_GROUP * HEAD_DIM
N_GROUPS = N_HEADS // HEADS_PER_GROUP
MOBA_GROUPS_PER_STEP = 4
SCORE_LOOKAHEAD = 3
SOFTMAX_M_INIT = -1e30
SUBLANES = 8
TOKEN_TILE = 512
FF_CHUNK = 256
VMEM_LIMIT_BYTES = 56 * 1024 * 1024

_NT_DIMS = (((1,), (1,)), ((), ()))
_F32 = jnp.float32
_BF16 = jnp.bfloat16


def _resident(shape):
    nd = len(shape)
    return pl.BlockSpec(shape, lambda *_: (0,) * nd, pipeline_mode=pl.Buffered(1))


def _shifted_rows(buf_ref, cur, carry, tile):
    buf_ref[0:SUBLANES, :] = carry
    buf_ref[SUBLANES:SUBLANES + tile, :] = cur
    prev2 = buf_ref[SUBLANES - 2:SUBLANES - 2 + tile, :]
    prev1 = buf_ref[SUBLANES - 1:SUBLANES - 1 + tile, :]
    return prev2, prev1


def _layer_norm(z, g, b):
    mu = jnp.mean(z, axis=-1, keepdims=True)
    zc = z - mu
    var = jnp.mean(zc * zc, axis=-1, keepdims=True)
    return zc * lax.rsqrt(var + LN_EPS) * g + b


def _inproj_kernel(x_ref, pos_ref, invf_ref, wqkvt_ref, wrest_ref, bg_ref, cw_ref, wcnv_ref,
                   qt_ref, k_ref, vt_ref, gatt_ref, cnvg_ref, buf_ref, carry_ref):
    tile = x_ref.shape[1]
    d_model = x_ref.shape[2]
    xb = x_ref[0].astype(_BF16)

    qkvt = lax.dot_general(wqkvt_ref[...], xb, _NT_DIMS, preferred_element_type=_F32)
    ang = invf_ref[...] * pos_ref[0].astype(_F32)
    cos, sin = jnp.cos(ang), jnp.sin(ang)

    def rope_t(t):
        parts = []
        for h in range(N_HEADS):
            r = h * HEAD_DIM
            x1 = t[r:r + ROT_HALF]
            x2 = t[r + ROT_HALF:r + ROT_DIM]
            parts += [x1 * cos - x2 * sin, x2 * cos + x1 * sin, t[r + ROT_DIM:r + HEAD_DIM]]
        return jnp.concatenate(parts, axis=0)

    a = ATTN_WIDTH
    qt_ref[0] = (rope_t(qkvt[0:a]) * Q_SCALE).astype(_BF16)
    k_ref[0] = rope_t(qkvt[a:2 * a]).T.astype(_BF16)
    vt = qkvt[2 * a:3 * a].astype(_BF16)
    for j in range(tile // MOBA_BLOCK):
        vt_ref[0, j] = vt[:, j * MOBA_BLOCK:(j + 1) * MOBA_BLOCK]

    rest = jnp.dot(xb, wrest_ref[...], preferred_element_type=_F32)
    c = CONV_WIDTH
    gate_b, gate_c, h_in = rest[:, 0:c], rest[:, c:2 * c], rest[:, 2 * c:3 * c]
    ch = gate_c * h_in

    @pl.when(pl.program_id(1) == 0)
    def _():
        carry_ref[...] = jnp.zeros_like(carry_ref)

    prev2, prev1 = _shifted_rows(buf_ref, ch, carry_ref[...], tile)
    carry_ref[...] = ch[tile - SUBLANES:tile]
    cw = cw_ref[...]
    conv = cw[0:1] * prev2 + cw[1:2] * prev1 + cw[2:3] * ch
    y_cnv = jnp.dot((gate_b * conv).astype(_BF16), wcnv_ref[...], preferred_element_type=_F32)

    gates = jax.nn.sigmoid(rest[:, 3 * c:] + bg_ref[...])
    gatt_ref[0] = gates[:, :d_model].astype(_BF16)
    cnvg_ref[0] = (gates[:, d_model:] * y_cnv).astype(_BF16)


def _inproj(x, pos3, inv_freq, wqkvt, wrest, bg, cw, wcnv):
    bsz, s, d = x.shape
    tile = TOKEN_TILE
    nb = s // MOBA_BLOCK
    a = ATTN_WIDTH
    tok = lambda w: pl.BlockSpec((1, tile, w), lambda b, t: (b, t, 0))
    return pl.pallas_call(
        _inproj_kernel,
        grid=(bsz, s // tile),
        in_specs=[
            tok(d),
            pl.BlockSpec((1, 1, tile), lambda b, t: (b, 0, t)),
            _resident(inv_freq.shape),
            _resident(wqkvt.shape),
            _resident(wrest.shape),
            _resident(bg.shape),
            _resident(cw.shape),
            _resident(wcnv.shape),
        ],
        out_specs=[
            pl.BlockSpec((1, a, tile), lambda b, t: (b, 0, t)),
            tok(a),
            pl.BlockSpec((1, tile // MOBA_BLOCK, a, MOBA_BLOCK), lambda b, t: (b, t, 0, 0)),
            tok(d),
            tok(d),
        ],
        out_shape=[
            jax.ShapeDtypeStruct((bsz, a, s), _BF16),
            jax.ShapeDtypeStruct((bsz, s, a), _BF16),
            jax.ShapeDtypeStruct((bsz, nb, a, MOBA_BLOCK), _BF16),
            jax.ShapeDtypeStruct((bsz, s, d), _BF16),
            jax.ShapeDtypeStruct((bsz, s, d), _BF16),
        ],
        scratch_shapes=[
            pltpu.VMEM((tile + SUBLANES, CONV_WIDTH), _F32),
            pltpu.VMEM((SUBLANES, CONV_WIDTH), _F32),
        ],
        compiler_params=pltpu.CompilerParams(
            dimension_semantics=("arbitrary", "arbitrary"), vmem_limit_bytes=VMEM_LIMIT_BYTES),
        name="inproj",
    )(x, pos3, inv_freq, wqkvt, wrest, bg, cw, wcnv)


def _moba_kernel(qt_ref, k_ref, vt_ref, o_ref, kmean_ref, mrow_ref, qm_ref, s_ref, p_ref, acc_ref,
                 m_ref, l_ref, alpha_ref):
    i = pl.program_id(2)
    nb = vt_ref.shape[1]
    blk = MOBA_BLOCK
    neg_inf = -jnp.inf

    @pl.when(i == 0)
    def _():
        def mean_body(j, _):
            kb = k_ref[0, pl.ds(pl.multiple_of(j * blk, blk), blk), :].astype(_F32)
            kmean_ref[pl.ds(j, 1), :] = jnp.sum(kb, axis=0, keepdims=True) / blk
            return 0
        lax.fori_loop(0, nb, mean_body, 0)

    n_heads = qt_ref.shape[1] // HEAD_DIM
    gw = GROUP_WIDTH
    grow = lax.broadcasted_iota(jnp.int32, (gw, blk), 0)
    for h in range(n_heads):
        g, hh = divmod(h, HEADS_PER_GROUP)
        qg = qt_ref[0, g * gw:(g + 1) * gw, :]
        qm_ref[h] = jnp.where((grow >= hh * HEAD_DIM) & (grow < (hh + 1) * HEAD_DIM), qg, jnp.zeros_like(qg))
    gsl = lambda h: slice((h // HEADS_PER_GROUP) * gw, (h // HEADS_PER_GROUP + 1) * gw)

    kmean = kmean_ref[...].astype(_BF16)
    brow = lax.broadcasted_iota(jnp.int32, (nb, blk), 0)
    for h in range(n_heads):
        sc = jnp.dot(kmean[:, gsl(h)], qm_ref[h], preferred_element_type=_F32)
        sc = jnp.where(brow < i, sc, neg_inf)
        sel = jnp.zeros((nb, blk), dtype=jnp.bool_)
        for _ in range(MOBA_TOPK):
            best = jnp.max(sc, axis=0, keepdims=True)
            idx = jnp.min(jnp.where(sc == best, brow, nb), axis=0, keepdims=True)
            pick = brow == idx
            sel = sel | pick
            sc = jnp.where(pick, neg_inf, sc)
        sel = sel & (brow < i)
        mrow_ref[h] = jnp.where(sel, 0.0, jnp.inf).astype(_F32)

    last = n_heads - 1
    for h in range(n_heads):
        m_ref[h] = jnp.full((1, blk), SOFTMAX_M_INIT, _F32)
        l_ref[h] = jnp.zeros((1, blk), _F32)
        acc_ref[h] = jnp.zeros((HEAD_DIM, blk), _F32)
    p_ref[last] = jnp.zeros((blk, blk), _BF16)
    alpha_ref[last] = jnp.ones((1, blk), _F32)

    kpos = lax.broadcasted_iota(jnp.int32, (blk, blk), 0)
    qpos = lax.broadcasted_iota(jnp.int32, (blk, blk), 1)

    def scores(j, h):
        kj = k_ref[0, pl.ds(pl.multiple_of(j * blk, blk), blk), gsl(h)]
        s_ref[h] = jnp.dot(kj, qm_ref[h], preferred_element_type=_F32)

    def softmax_unit(j, h, own):
        s = s_ref[h]
        m_old = m_ref[h]
        if own:
            s = jnp.where(kpos <= qpos, s, neg_inf)
            m_new = jnp.maximum(m_old, jnp.max(s, axis=0, keepdims=True))
            shift = m_new
        else:
            off = mrow_ref[h, pl.ds(j, 1), :]
            m_new = jnp.maximum(m_old, jnp.max(s, axis=0, keepdims=True) - off)
            shift = m_new + off
        alpha = jnp.exp2(m_old - m_new)
        p = jnp.exp2(s - shift)
        l_ref[h] = alpha * l_ref[h] + jnp.sum(p, axis=0, keepdims=True)
        m_ref[h] = m_new
        alpha_ref[h] = alpha
        p_ref[h] = p.astype(_BF16)

    def pv_unit(j, h):
        pv = jnp.dot(vt_ref[0, j, h * HEAD_DIM:(h + 1) * HEAD_DIM, :], p_ref[h], preferred_element_type=_F32)
        acc_ref[h] = alpha_ref[h] * acc_ref[h] + pv

    def process_block(j, j_next, own):
        for h in range(n_heads):
            ahead = h + SCORE_LOOKAHEAD
            if ahead < n_heads:
                scores(j, ahead)
            elif j_next is not None:
                scores(j_next, ahead - n_heads)
            softmax_unit(j, h, own)
            if h == 0:
                pv_unit(jnp.maximum(j - 1, 0), last)
            else:
                pv_unit(j, h - 1)

    for h in range(SCORE_LOOKAHEAD):
        scores(0, h)

    def blk_body(j, carry):
        process_block(j, j + 1, False)
        return carry

    lax.fori_loop(0, i, blk_body, 0)
    process_block(i, None, True)
    pv_unit(i, last)

    ot = jnp.concatenate([acc_ref[h] / l_ref[h] for h in range(n_heads)], axis=0)
    o_ref[0] = ot.T.astype(_BF16)


def _moba(qt, k, vt):
    bsz, a, s = qt.shape
    nb = s // MOBA_BLOCK
    gw = GROUP_WIDTH * MOBA_GROUPS_PER_STEP
    n_heads = HEADS_PER_GROUP * MOBA_GROUPS_PER_STEP
    assert SCORE_LOOKAHEAD < n_heads
    return pl.pallas_call(
        _moba_kernel,
        grid=(bsz, N_GROUPS // MOBA_GROUPS_PER_STEP, nb),
        in_specs=[
            pl.BlockSpec((1, gw, MOBA_BLOCK), lambda b, g, i: (b, g, i)),
            pl.BlockSpec((1, s, gw), lambda b, g, i: (b, 0, g)),
            pl.BlockSpec((1, nb, gw, MOBA_BLOCK), lambda b, g, i: (b, 0, g, 0)),
        ],
        out_specs=pl.BlockSpec((1, MOBA_BLOCK, gw), lambda b, g, i: (b, i, g)),
        out_shape=jax.ShapeDtypeStruct((bsz, s, a), _BF16),
        scratch_shapes=[
            pltpu.VMEM((nb, gw), _F32),
            pltpu.VMEM((n_heads, nb, MOBA_BLOCK), _F32),
            pltpu.VMEM((n_heads, GROUP_WIDTH, MOBA_BLOCK), _BF16),
            pltpu.VMEM((n_heads, MOBA_BLOCK, MOBA_BLOCK), _F32),
            pltpu.VMEM((n_heads, MOBA_BLOCK, MOBA_BLOCK), _BF16),
            pltpu.VMEM((n_heads, HEAD_DIM, MOBA_BLOCK), _F32),
            pltpu.VMEM((n_heads, 1, MOBA_BLOCK), _F32),
            pltpu.VMEM((n_heads, 1, MOBA_BLOCK), _F32),
            pltpu.VMEM((n_heads, 1, MOBA_BLOCK), _F32),
        ],
        compiler_params=pltpu.CompilerParams(
            dimension_semantics=("arbitrary", "arbitrary", "arbitrary"), vmem_limit_bytes=VMEM_LIMIT_BYTES),
        name="moba",
    )(qt, k, vt)


def _post_kernel(x_ref, attn_ref, gatt_ref, cnvg_ref, watt_ref, wo_ref, g1_ref, b1_ref,
                 wupg_ref, wupv_ref, cwg_ref, cwv_ref, wdown_ref, g2_ref, b2_ref,
                 o_ref, buf_ref, carryg_ref, carryv_ref, hmid_ref):
    tile = x_ref.shape[1]
    n_chunks = wupg_ref.shape[0]

    y_att = jnp.dot(attn_ref[0], watt_ref[...], preferred_element_type=_F32)
    t = gatt_ref[0].astype(_F32) * y_att + cnvg_ref[0].astype(_F32)
    mix = jnp.dot(t.astype(_BF16), wo_ref[...], preferred_element_type=_F32)
    x1 = _layer_norm(DEEPNORM_ALPHA * x_ref[0] + mix, g1_ref[...], b1_ref[...])
    x1b = x1.astype(_BF16)

    @pl.when(pl.program_id(1) == 0)
    def _():
        carryg_ref[...] = jnp.zeros_like(carryg_ref)
        carryv_ref[...] = jnp.zeros_like(carryv_ref)

    def conv(u, cw_ref, carry_ref, c):
        prev2, prev1 = _shifted_rows(buf_ref, u, carry_ref[c], tile)
        carry_ref[c] = u[tile - SUBLANES:tile]
        cw = cw_ref[c]
        return cw[0:1] * prev2 + cw[1:2] * prev1 + cw[2:3] * u

    def chunk_body(c, _):
        ug = jnp.dot(x1b, wupg_ref[c], preferred_element_type=_F32)
        uv = jnp.dot(x1b, wupv_ref[c], preferred_element_type=_F32)
        cg = conv(ug, cwg_ref, carryg_ref, c)
        cv = conv(uv, cwv_ref, carryv_ref, c)
        hmid_ref[c] = (cg * jax.nn.sigmoid(cg) * cv).astype(_BF16)
        return 0

    lax.fori_loop(0, n_chunks, chunk_body, 0)

    f = jnp.dot(hmid_ref[0], wdown_ref[0], preferred_element_type=_F32)
    for c in range(1, n_chunks):
        f = f + jnp.dot(hmid_ref[c], wdown_ref[c], preferred_element_type=_F32)
    o_ref[0] = _layer_norm(DEEPNORM_ALPHA * x1 + f, g2_ref[...], b2_ref[...])


def _post(x, attn, gatt, cnvg, watt, wo, g1, b1, wupg, wupv, cwg, cwv, wdown, g2, b2):
    bsz, s, d = x.shape
    tile = TOKEN_TILE
    n_chunks = wupg.shape[0]
    tok = lambda w: pl.BlockSpec((1, tile, w), lambda b, t: (b, t, 0))
    weights = (watt, wo, g1, b1, wupg, wupv, cwg, cwv, wdown, g2, b2)
    return pl.pallas_call(
        _post_kernel,
        grid=(bsz, s // tile),
        in_specs=[tok(d), tok(ATTN_WIDTH), tok(d), tok(d)] + [_resident(w.shape) for w in weights],
        out_specs=tok(d),
        out_shape=jax.ShapeDtypeStruct((bsz, s, d), x.dtype),
        scratch_shapes=[
            pltpu.VMEM((tile + SUBLANES, FF_CHUNK), _F32),
            pltpu.VMEM((n_chunks, SUBLANES, FF_CHUNK), _F32),
            pltpu.VMEM((n_chunks, SUBLANES, FF_CHUNK), _F32),
            pltpu.VMEM((n_chunks, tile, FF_CHUNK), _BF16),
        ],
        compiler_params=pltpu.CompilerParams(
            dimension_semantics=("arbitrary", "arbitrary"), vmem_limit_bytes=VMEM_LIMIT_BYTES),
        name="post",
    )(x, attn, gatt, cnvg, *weights)


def _chunk_cols(w, n_chunks):
    return w.reshape(w.shape[0], n_chunks, FF_CHUNK).transpose(1, 0, 2)


def kernel(x, positions, w_in, b_gate, w_attn_out, conv_w_mix, w_conv_out, w_o,
           ln1_g, ln1_b, w_up, conv_w_ffn, w_down, ln2_g, ln2_b):
    bsz, s, d = x.shape
    assert s % TOKEN_TILE == 0 and TOKEN_TILE % MOBA_BLOCK == 0 and D_FF % FF_CHUNK == 0
    a = ATTN_WIDTH
    n_chunks = D_FF // FF_CHUNK
    inv_freq = (ROPE_THETA ** (-jnp.arange(0, ROT_DIM, 2, dtype=_F32) / ROT_DIM)).reshape(ROT_HALF, 1)
    pos3 = positions.reshape(bsz, 1, s)
    for l in range(DEPTH):
        wqkvt = w_in[l, :, :3 * a].T.astype(_BF16)
        wrest = w_in[l, :, 3 * a:].astype(_BF16)
        qt, k, vt, gatt, cnvg = _inproj(
            x, pos3, inv_freq, wqkvt, wrest, b_gate[l].reshape(1, -1), conv_w_mix[l],
            w_conv_out[l].astype(_BF16))
        attn = _moba(qt, k, vt)
        row = lambda v: v.reshape(1, -1)
        x = _post(
            x, attn, gatt, cnvg, w_attn_out[l].astype(_BF16), w_o[l].astype(_BF16),
            row(ln1_g[l]), row(ln1_b[l]),
            _chunk_cols(w_up[l, :, :D_FF].astype(_BF16), n_chunks),
            _chunk_cols(w_up[l, :, D_FF:].astype(_BF16), n_chunks),
            _chunk_cols(conv_w_ffn[l, :, :D_FF], n_chunks),
            _chunk_cols(conv_w_ffn[l, :, D_FF:], n_chunks),
            w_down[l].astype(_BF16).reshape(n_chunks, FF_CHUNK, d),
            row(ln2_g[l]), row(ln2_b[l]))
    return x
```

```python
import functools
import math

import jax
import jax.numpy as jnp
from jax import lax
from jax.experimental import pallas as pl
from jax.experimental.pallas import tpu as pltpu

N_HEADS = 8
HEAD_DIM = 64
ATTN_WIDTH = N_HEADS * HEAD_DIM
MOBA_BLOCK = 256
MOBA_TOPK = 3
ROPE_THETA = 500000.0
ROT_DIM = HEAD_DIM // 4
ROT_HALF = ROT_DIM // 2
CONV_WIDTH = 512
CONV_K = 3
D_FF = 2816
DEPTH = 1
DEEPNORM_ALPHA = (2.0 * DEPTH) ** 0.25
LN_EPS = 1e-5

Q_SCALE = (1.0 / math.sqrt(HEAD_DIM)) * math.log2(math.e)

HEADS_PER_GROUP = 2
GROUP_WIDTH = HEADS_PER_GROUP * HEAD_DIM
N_GROUPS = N_HEADS // HEADS_PER_GROUP
MOBA_GROUPS_PER_STEP = 4
SCORE_LOOKAHEAD = 4
SOFTMAX_M_INIT = -1e30
BF16_SUBLANES = 16
ACC_ROWS = HEAD_DIM + BF16_SUBLANES
SUBLANES = 8
TOKEN_TILE = 512
FF_CHUNK = 256
VMEM_LIMIT_BYTES = 56 * 1024 * 1024

_NT_DIMS = (((1,), (1,)), ((), ()))
_F32 = jnp.float32
_BF16 = jnp.bfloat16


def _resident(shape):
    nd = len(shape)
    return pl.BlockSpec(shape, lambda *_: (0,) * nd, pipeline_mode=pl.Buffered(1))


def _causal_conv3(cur, carry, w):
    row = lax.broadcasted_iota(jnp.int32, carry.shape, 0)
    out = w[2:3] * cur
    for shift in (1, 2):
        rolled = pltpu.roll(cur, shift, axis=0)
        top = jnp.where(row < shift, pltpu.roll(carry, shift, axis=0), rolled[0:SUBLANES])
        prev = jnp.concatenate([top, rolled[SUBLANES:]], axis=0)
        out = out + w[2 - shift:3 - shift] * prev
    return out


def _layer_norm(z, g, b):
    mu = jnp.mean(z, axis=-1, keepdims=True)
    zc = z - mu
    var = jnp.mean(zc * zc, axis=-1, keepdims=True)
    return zc * lax.rsqrt(var + LN_EPS) * g + b


def _inproj_kernel(x_ref, pos_ref, invf_ref, wqkvt_ref, wrest_ref, bg_ref, cw_ref, wcnv_ref,
                   qt_ref, k_ref, vt_ref, gatt_ref, cnvg_ref, carry_ref):
    tile = x_ref.shape[1]
    d_model = x_ref.shape[2]
    c = CONV_WIDTH
    xb = x_ref[0].astype(_BF16)

    gates_pre = jnp.dot(xb, wrest_ref[:, 3 * c:], preferred_element_type=_F32)
    qkvt = lax.dot_general(wqkvt_ref[...], xb, _NT_DIMS, preferred_element_type=_F32)
    gates = jax.nn.sigmoid(gates_pre + bg_ref[...])
    gatt_ref[0] = gates[:, :d_model].astype(_BF16)
    g_cnv = gates[:, d_model:]

    rest = jnp.dot(xb, wrest_ref[:, 0:3 * c], preferred_element_type=_F32)
    ang = invf_ref[...] * pos_ref[0].astype(_F32)
    cos, sin = jnp.cos(ang), jnp.sin(ang)

    def rope_t(t):
        parts = []
        for h in range(N_HEADS):
            r = h * HEAD_DIM
            x1 = t[r:r + ROT_HALF]
            x2 = t[r + ROT_HALF:r + ROT_DIM]
            parts += [x1 * cos - x2 * sin, x2 * cos + x1 * sin, t[r + ROT_DIM:r + HEAD_DIM]]
        return jnp.concatenate(parts, axis=0)

    a = ATTN_WIDTH
    qt_ref[0] = (rope_t(qkvt[0:a]) * Q_SCALE).astype(_BF16)
    k_ref[0] = rope_t(qkvt[a:2 * a]).T.astype(_BF16)
    vt = qkvt[2 * a:3 * a].astype(_BF16)
    for j in range(tile // MOBA_BLOCK):
        vt_ref[0, j] = vt[:, j * MOBA_BLOCK:(j + 1) * MOBA_BLOCK]

    gate_b, gate_c, h_in = rest[:, 0:c], rest[:, c:2 * c], rest[:, 2 * c:3 * c]
    ch = gate_c * h_in

    @pl.when(pl.program_id(1) == 0)
    def _():
        carry_ref[...] = jnp.zeros_like(carry_ref)

    conv = _causal_conv3(ch, carry_ref[...], cw_ref[...])
    carry_ref[...] = ch[tile - SUBLANES:tile]
    y_cnv = jnp.dot((gate_b * conv).astype(_BF16), wcnv_ref[...], preferred_element_type=_F32)
    cnvg_ref[0] = (g_cnv * y_cnv).astype(_BF16)


def _inproj(x, pos3, inv_freq, wqkvt, wrest, bg, cw, wcnv):
    bsz, s, d = x.shape
    tile = TOKEN_TILE
    nb = s // MOBA_BLOCK
    a = ATTN_WIDTH
    tok = lambda w: pl.BlockSpec((1, tile, w), lambda b, t: (b, t, 0))
    return pl.pallas_call(
        _inproj_kernel,
        grid=(bsz, s // tile),
        in_specs=[
            tok(d),
            pl.BlockSpec((1, 1, tile), lambda b, t: (b, 0, t)),
            _resident(inv_freq.shape),
            _resident(wqkvt.shape),
            _resident(wrest.shape),
            _resident(bg.shape),
            _resident(cw.shape),
            _resident(wcnv.shape),
        ],
        out_specs=[
            pl.BlockSpec((1, a, tile), lambda b, t: (b, 0, t)),
            tok(a),
            pl.BlockSpec((1, tile // MOBA_BLOCK, a, MOBA_BLOCK), lambda b, t: (b, t, 0, 0)),
            tok(d),
            tok(d),
        ],
        out_shape=[
            jax.ShapeDtypeStruct((bsz, a, s), _BF16),
            jax.ShapeDtypeStruct((bsz, s, a), _BF16),
            jax.ShapeDtypeStruct((bsz, nb, a, MOBA_BLOCK), _BF16),
            jax.ShapeDtypeStruct((bsz, s, d), _BF16),
            jax.ShapeDtypeStruct((bsz, s, d), _BF16),
        ],
        scratch_shapes=[pltpu.VMEM((SUBLANES, CONV_WIDTH), _F32)],
        compiler_params=pltpu.CompilerParams(
            dimension_semantics=("arbitrary", "arbitrary"), vmem_limit_bytes=VMEM_LIMIT_BYTES),
        name="inproj",
    )(x, pos3, inv_freq, wqkvt, wrest, bg, cw, wcnv)


def _moba_kernel(qt_ref, k_ref, vt_ref, o_ref, kmean_ref, mrow_ref, qm_ref, s_ref, p_ref, acc_ref,
                 m_ref, mnew_ref, shift_ref, alpha_ref):
    i = pl.program_id(2)
    nb = vt_ref.shape[1]
    blk = MOBA_BLOCK
    neg_inf = -jnp.inf

    @pl.when(i == 0)
    def _():
        def mean_body(j, _):
            kb = k_ref[0, pl.ds(pl.multiple_of(j * blk, blk), blk), :].astype(_F32)
            kmean_ref[pl.ds(j, 1), :] = jnp.sum(kb, axis=0, keepdims=True) / blk
            return 0
        lax.fori_loop(0, nb, mean_body, 0)

    n_heads = qt_ref.shape[1] // HEAD_DIM
    gw = GROUP_WIDTH
    grow = lax.broadcasted_iota(jnp.int32, (gw, blk), 0)
    for h in range(n_heads):
        g, hh = divmod(h, HEADS_PER_GROUP)
        qg = qt_ref[0, g * gw:(g + 1) * gw, :]
        qm_ref[h] = jnp.where((grow >= hh * HEAD_DIM) & (grow < (hh + 1) * HEAD_DIM), qg, jnp.zeros_like(qg))
    gsl = lambda h: slice((h // HEADS_PER_GROUP) * gw, (h // HEADS_PER_GROUP + 1) * gw)

    kmean = kmean_ref[...].astype(_BF16)
    brow = lax.broadcasted_iota(jnp.int32, (nb, blk), 0)
    for h in range(n_heads):
        sc = jnp.dot(kmean[:, gsl(h)], qm_ref[h], preferred_element_type=_F32)
        sc = jnp.where(brow < i, sc, neg_inf)
        sel = jnp.zeros((nb, blk), dtype=jnp.bool_)
        for _ in range(MOBA_TOPK):
            best = jnp.max(sc, axis=0, keepdims=True)
            idx = jnp.min(jnp.where(sc == best, brow, nb), axis=0, keepdims=True)
            pick = brow == idx
            sel = sel | pick
            sc = jnp.where(pick, neg_inf, sc)
        sel = sel & (brow < i)
        mrow_ref[h] = jnp.where(sel, 0.0, jnp.inf).astype(_F32)

    last = n_heads - 1
    for h in range(n_heads):
        m_ref[h] = jnp.full((1, blk), SOFTMAX_M_INIT, _F32)
        acc_ref[h] = jnp.zeros((ACC_ROWS, blk), _F32)
    p_ref[last] = jnp.zeros((blk, blk), _BF16)
    alpha_ref[last] = jnp.ones((1, blk), _F32)

    kpos = lax.broadcasted_iota(jnp.int32, (blk, blk), 0)
    qpos = lax.broadcasted_iota(jnp.int32, (blk, blk), 1)
    ones_rows = jnp.ones((ACC_ROWS - HEAD_DIM, blk), _BF16)

    def scores(j, h):
        kj = k_ref[0, pl.ds(pl.multiple_of(j * blk, blk), blk), gsl(h)]
        s_ref[h] = jnp.dot(kj, qm_ref[h], preferred_element_type=_F32)

    def load_scores(h, own):
        s = s_ref[h]
        return jnp.where(kpos <= qpos, s, neg_inf) if own else s

    def stats(j, h, own):
        cmax = jnp.max(load_scores(h, own), axis=0, keepdims=True)
        m_old = m_ref[h]
        if own:
            m_new = jnp.maximum(m_old, cmax)
            shift_ref[h] = m_new
        else:
            off = mrow_ref[h, pl.ds(j, 1), :]
            m_new = jnp.maximum(m_old, cmax - off)
            shift_ref[h] = m_new + off
        mnew_ref[h] = m_new
        alpha_ref[h] = jnp.exp2(m_old - m_new)

    def expo(h, own):
        p_ref[h] = jnp.exp2(load_scores(h, own) - shift_ref[h]).astype(_BF16)
        m_ref[h] = mnew_ref[h]

    def pv_unit(j, h):
        vt1 = jnp.concatenate([vt_ref[0, j, h * HEAD_DIM:(h + 1) * HEAD_DIM, :], ones_rows], axis=0)
        pv = jnp.dot(vt1, p_ref[h], preferred_element_type=_F32)
        acc_ref[h] = alpha_ref[h] * acc_ref[h] + pv

    def process_block(j, own, j_next, j_prev):
        for h in range(n_heads):
            ahead = h + SCORE_LOOKAHEAD
            if ahead < n_heads:
                scores(j, ahead)
            else:
                scores(j_next, ahead - n_heads)
            if h + 1 < n_heads:
                stats(j, h + 1, own)
            else:
                stats(j_next, 0, False)
            expo(h, own)
            if h == 0:
                pv_unit(j_prev, last)
            else:
                pv_unit(j, h - 1)

    for h in range(SCORE_LOOKAHEAD):
        scores(i, h)
    stats(i, 0, True)
    process_block(i, True, 0, i)

    def past_block(j):
        process_block(j, False, jnp.minimum(j + 1, nb - 1), jnp.where(j == 0, i, j - 1))

    def pair_body(jj, carry):
        past_block(2 * jj)
        past_block(2 * jj + 1)
        return carry

    def single_body(j, carry):
        past_block(j)
        return carry

    n_pairs = lax.shift_right_logical(i, 1)
    lax.fori_loop(0, n_pairs, pair_body, 0)
    lax.fori_loop(2 * n_pairs, i, single_body, 0)
    pv_unit(jnp.where(i == 0, i, i - 1), last)

    ot = jnp.concatenate(
        [acc_ref[h, 0:HEAD_DIM, :] / acc_ref[h, HEAD_DIM:HEAD_DIM + 1, :] for h in range(n_heads)], axis=0)
    o_ref[0] = ot.T.astype(_BF16)


def _moba(qt, k, vt):
    bsz, a, s = qt.shape
    nb = s // MOBA_BLOCK
    gw = GROUP_WIDTH * MOBA_GROUPS_PER_STEP
    n_heads = HEADS_PER_GROUP * MOBA_GROUPS_PER_STEP
    assert SCORE_LOOKAHEAD < n_heads
    return pl.pallas_call(
        _moba_kernel,
        grid=(bsz, N_GROUPS // MOBA_GROUPS_PER_STEP, nb),
        in_specs=[
            pl.BlockSpec((1, gw, MOBA_BLOCK), lambda b, g, i: (b, g, i)),
            pl.BlockSpec((1, s, gw), lambda b, g, i: (b, 0, g)),
            pl.BlockSpec((1, nb, gw, MOBA_BLOCK), lambda b, g, i: (b, 0, g, 0)),
        ],
        out_specs=pl.BlockSpec((1, MOBA_BLOCK, gw), lambda b, g, i: (b, i, g)),
        out_shape=jax.ShapeDtypeStruct((bsz, s, a), _BF16),
        scratch_shapes=[
            pltpu.VMEM((nb, gw), _F32),
            pltpu.VMEM((n_heads, nb, MOBA_BLOCK), _F32),
            pltpu.VMEM((n_heads, GROUP_WIDTH, MOBA_BLOCK), _BF16),
            pltpu.VMEM((n_heads, MOBA_BLOCK, MOBA_BLOCK), _F32),
            pltpu.VMEM((n_heads, MOBA_BLOCK, MOBA_BLOCK), _BF16),
            pltpu.VMEM((n_heads, ACC_ROWS, MOBA_BLOCK), _F32),
            pltpu.VMEM((n_heads, 1, MOBA_BLOCK), _F32),
            pltpu.VMEM((n_heads, 1, MOBA_BLOCK), _F32),
            pltpu.VMEM((n_heads, 1, MOBA_BLOCK), _F32),
            pltpu.VMEM((n_heads, 1, MOBA_BLOCK), _F32),
        ],
        compiler_params=pltpu.CompilerParams(
            dimension_semantics=("arbitrary", "arbitrary", "arbitrary"), vmem_limit_bytes=VMEM_LIMIT_BYTES),
        name="moba",
    )(qt, k, vt)


def _post_kernel(x_ref, attn_ref, gatt_ref, cnvg_ref, watt_ref, wo_ref, g1_ref, b1_ref,
                 wupg_ref, wupv_ref, cwg_ref, cwv_ref, wdown_ref, g2_ref, b2_ref,
                 o_ref, x1_ref, x1b_ref, u_ref, carryg_ref, carryv_ref, hmid_ref):
    tile = x_ref.shape[1]
    n_chunks = wupg_ref.shape[0]

    y_att = jnp.dot(attn_ref[0], watt_ref[...], preferred_element_type=_F32)
    t = gatt_ref[0].astype(_F32) * y_att + cnvg_ref[0].astype(_F32)
    mix = jnp.dot(t.astype(_BF16), wo_ref[...], preferred_element_type=_F32)
    x1 = _layer_norm(DEEPNORM_ALPHA * x_ref[0] + mix, g1_ref[...], b1_ref[...])
    x1_ref[...] = x1
    x1b_ref[...] = x1.astype(_BF16)

    @pl.when(pl.program_id(1) == 0)
    def _():
        carryg_ref[...] = jnp.zeros_like(carryg_ref)
        carryv_ref[...] = jnp.zeros_like(carryv_ref)

    def up_proj(c):
        u_ref[c % 2, 0] = jnp.dot(x1b_ref[...], wupg_ref[c], preferred_element_type=_F32)
        u_ref[c % 2, 1] = jnp.dot(x1b_ref[...], wupv_ref[c], preferred_element_type=_F32)

    def conv(u, cw_ref, carry_ref, c):
        out = _causal_conv3(u, carry_ref[c], cw_ref[c])
        carry_ref[c] = u[tile - SUBLANES:tile]
        return out

    def gated_conv(c):
        cg = conv(u_ref[c % 2, 0], cwg_ref, carryg_ref, c)
        cv = conv(u_ref[c % 2, 1], cwv_ref, carryv_ref, c)
        hmid_ref[c] = (cg * jax.nn.sigmoid(cg) * cv).astype(_BF16)

    up_proj(0)
    for c in range(n_chunks):
        if c + 1 < n_chunks:
            up_proj(c + 1)
        gated_conv(c)

    f = jnp.dot(hmid_ref[0], wdown_ref[0], preferred_element_type=_F32)
    for c in range(1, n_chunks):
        f = f + jnp.dot(hmid_ref[c], wdown_ref[c], preferred_element_type=_F32)
    o_ref[0] = _layer_norm(DEEPNORM_ALPHA * x1_ref[...] + f, g2_ref[...], b2_ref[...])


def _post(x, attn, gatt, cnvg, watt, wo, g1, b1, wupg, wupv, cwg, cwv, wdown, g2, b2):
    bsz, s, d = x.shape
    tile = TOKEN_TILE
    n_chunks = wupg.shape[0]
    tok = lambda w: pl.BlockSpec((1, tile, w), lambda b, t: (b, t, 0))
    weights = (watt, wo, g1, b1, wupg, wupv, cwg, cwv, wdown, g2, b2)
    return pl.pallas_call(
        _post_kernel,
        grid=(bsz, s // tile),
        in_specs=[tok(d), tok(ATTN_WIDTH), tok(d), tok(d)] + [_resident(w.shape) for w in weights],
        out_specs=tok(d),
        out_shape=jax.ShapeDtypeStruct((bsz, s, d), x.dtype),
        scratch_shapes=[
            pltpu.VMEM((tile, d), _F32),
            pltpu.VMEM((tile, d), _BF16),
            pltpu.VMEM((2, 2, tile, FF_CHUNK), _F32),
            pltpu.VMEM((n_chunks, SUBLANES, FF_CHUNK), _F32),
            pltpu.VMEM((n_chunks, SUBLANES, FF_CHUNK), _F32),
            pltpu.VMEM((n_chunks, tile, FF_CHUNK), _BF16),
        ],
        compiler_params=pltpu.CompilerParams(
            dimension_semantics=("arbitrary", "arbitrary"), vmem_limit_bytes=VMEM_LIMIT_BYTES),
        name="post",
    )(x, attn, gatt, cnvg, *weights)


def _chunk_cols(w, n_chunks):
    return w.reshape(w.shape[0], n_chunks, FF_CHUNK).transpose(1, 0, 2)


def kernel(x, positions, w_in, b_gate, w_attn_out, conv_w_mix, w_conv_out, w_o,
           ln1_g, ln1_b, w_up, conv_w_ffn, w_down, ln2_g, ln2_b):
    bsz, s, d = x.shape
    assert s % TOKEN_TILE == 0 and TOKEN_TILE % MOBA_BLOCK == 0 and D_FF % FF_CHUNK == 0
    a = ATTN_WIDTH
    n_chunks = D_FF // FF_CHUNK
    inv_freq = (ROPE_THETA ** (-jnp.arange(0, ROT_DIM, 2, dtype=_F32) / ROT_DIM)).reshape(ROT_HALF, 1)
    pos3 = positions.reshape(bsz, 1, s)
    for l in range(DEPTH):
        wqkvt = w_in[l, :, :3 * a].T.astype(_BF16)
        wrest = w_in[l, :, 3 * a:].astype(_BF16)
        qt, k, vt, gatt, cnvg = _inproj(
            x, pos3, inv_freq, wqkvt, wrest, b_gate[l].reshape(1, -1), conv_w_mix[l],
            w_conv_out[l].astype(_BF16))
        attn = _moba(qt, k, vt)
        row = lambda v: v.reshape(1, -1)
        x = _post(
            x, attn, gatt, cnvg, w_attn_out[l].astype(_BF16), w_o[l].astype(_BF16),
            row(ln1_g[l]), row(ln1_b[l]),
            _chunk_cols(w_up[l, :, :D_FF].astype(_BF16), n_chunks),
            _chunk_cols(w_up[l, :, D_FF:].astype(_BF16), n_chunks),
            _chunk_cols(conv_w_ffn[l, :, :D_FF], n_chunks),
            _chunk_cols(conv_w_ffn[l, :, D_FF:], n_chunks),
            w_down[l].astype(_BF16).reshape(n_chunks, FF_CHUNK, d),
            row(ln2_g[l]), row(ln2_b[l]))
    return x
```

```python
import functools
import math

import jax
import jax.numpy as jnp
from jax import lax
from jax.experimental import pallas as pl
from jax.experimental.pallas import tpu as pltpu

N_HEADS = 8
HEAD_DIM = 64
ATTN_WIDTH = N_HEADS * HEAD_DIM
MOBA_BLOCK = 256
MOBA_TOPK = 3
ROPE_THETA = 500000.0
ROT_DIM = HEAD_DIM // 4
ROT_HALF = ROT_DIM // 2
CONV_WIDTH = 512
CONV_K = 3
D_FF = 2816
DEPTH = 1
DEEPNORM_ALPHA = (2.0 * DEPTH) ** 0.25
LN_EPS = 1e-5

Q_SCALE = (1.0 / math.sqrt(HEAD_DIM)) * math.log2(math.e)

HEADS_PER_GROUP = 2
GROUP_WIDTH = HEADS_PER_GROUP * HEAD_DIM
N_GROUPS = N_HEADS // HEADS_PER_GROUP
MOBA_GROUPS_PER_STEP = 4
SCORE_LOOKAHEAD = 4
SOFTMAX_M_INIT = -1e30
BF16_SUBLANES = 16
ACC_ROWS = HEAD_DIM + BF16_SUBLANES
SUBLANES = 8
TOKEN_TILE = 512
FF_CHUNK = 256
VMEM_LIMIT_BYTES = 56 * 1024 * 1024

_NT_DIMS = (((1,), (1,)), ((), ()))
_F32 = jnp.float32
_BF16 = jnp.bfloat16


def _resident(shape):
    nd = len(shape)
    return pl.BlockSpec(shape, lambda *_: (0,) * nd, pipeline_mode=pl.Buffered(1))


def _causal_conv3(cur, carry, w):
    first = lax.broadcasted_iota(jnp.int32, carry.shape, 0) == 0
    last1 = pltpu.roll(carry, 1, axis=0)
    last2 = pltpu.roll(carry, 2, axis=0)

    def shift1(a, a_before):
        rolled = pltpu.roll(a, 1, axis=0)
        return jnp.concatenate([jnp.where(first, a_before, rolled[0:SUBLANES]), rolled[SUBLANES:]], axis=0)

    inner = w[1:2] * cur + shift1(w[0:1] * cur, w[0:1] * last1)
    return w[2:3] * cur + shift1(inner, w[1:2] * last1 + w[0:1] * last2)


def _sigmoid(x):
    return 0.5 * jnp.tanh(0.5 * x) + 0.5


def _silu_times(g, v):
    h = 0.5 * g
    return h * (jnp.tanh(h) + 1.0) * v


def _layer_norm(z, g, b):
    mu = jnp.mean(z, axis=-1, keepdims=True)
    zc = z - mu
    var = jnp.mean(zc * zc, axis=-1, keepdims=True)
    return zc * lax.rsqrt(var + LN_EPS) * g + b


def _inproj_kernel(x_ref, pos_ref, invf_ref, wqkvt_ref, wrest_ref, bg_ref, cw_ref, wcnv_ref,
                   qt_ref, k_ref, vt_ref, gatt_ref, cnvg_ref, carry_ref):
    tile = x_ref.shape[1]
    d_model = x_ref.shape[2]
    c = CONV_WIDTH
    xb = x_ref[0].astype(_BF16)

    gates_pre = jnp.dot(xb, wrest_ref[:, 3 * c:], preferred_element_type=_F32)
    qkvt = lax.dot_general(wqkvt_ref[...], xb, _NT_DIMS, preferred_element_type=_F32)
    gates = _sigmoid(gates_pre + bg_ref[...])
    gatt_ref[0] = gates[:, :d_model].astype(_BF16)
    g_cnv = gates[:, d_model:]

    rest = jnp.dot(xb, wrest_ref[:, 0:3 * c], preferred_element_type=_F32)
    ang = invf_ref[...] * pos_ref[0].astype(_F32)
    cos, sin = jnp.cos(ang), jnp.sin(ang)

    def rope_t(t):
        parts = []
        for h in range(N_HEADS):
            r = h * HEAD_DIM
            x1 = t[r:r + ROT_HALF]
            x2 = t[r + ROT_HALF:r + ROT_DIM]
            parts += [x1 * cos - x2 * sin, x2 * cos + x1 * sin, t[r + ROT_DIM:r + HEAD_DIM]]
        return jnp.concatenate(parts, axis=0)

    a = ATTN_WIDTH
    qt_ref[0] = (rope_t(qkvt[0:a]) * Q_SCALE).astype(_BF16)
    k_ref[0] = rope_t(qkvt[a:2 * a]).T.astype(_BF16)
    vt = qkvt[2 * a:3 * a].astype(_BF16)
    for j in range(tile // MOBA_BLOCK):
        vt_ref[0, j] = vt[:, j * MOBA_BLOCK:(j + 1) * MOBA_BLOCK]

    gate_b, gate_c, h_in = rest[:, 0:c], rest[:, c:2 * c], rest[:, 2 * c:3 * c]
    ch = gate_c * h_in

    @pl.when(pl.program_id(1) == 0)
    def _():
        carry_ref[...] = jnp.zeros_like(carry_ref)

    conv = _causal_conv3(ch, carry_ref[...], cw_ref[...])
    carry_ref[...] = ch[tile - SUBLANES:tile]
    y_cnv = jnp.dot((gate_b * conv).astype(_BF16), wcnv_ref[...], preferred_element_type=_F32)
    cnvg_ref[0] = (g_cnv * y_cnv).astype(_BF16)


def _inproj(x, pos3, inv_freq, wqkvt, wrest, bg, cw, wcnv):
    bsz, s, d = x.shape
    tile = TOKEN_TILE
    nb = s // MOBA_BLOCK
    a = ATTN_WIDTH
    tok = lambda w: pl.BlockSpec((1, tile, w), lambda b, t: (b, t, 0))
    return pl.pallas_call(
        _inproj_kernel,
        grid=(bsz, s // tile),
        in_specs=[
            tok(d),
            pl.BlockSpec((1, 1, tile), lambda b, t: (b, 0, t)),
            _resident(inv_freq.shape),
            _resident(wqkvt.shape),
            _resident(wrest.shape),
            _resident(bg.shape),
            _resident(cw.shape),
            _resident(wcnv.shape),
        ],
        out_specs=[
            pl.BlockSpec((1, a, tile), lambda b, t: (b, 0, t)),
            tok(a),
            pl.BlockSpec((1, tile // MOBA_BLOCK, a, MOBA_BLOCK), lambda b, t: (b, t, 0, 0)),
            tok(d),
            tok(d),
        ],
        out_shape=[
            jax.ShapeDtypeStruct((bsz, a, s), _BF16),
            jax.ShapeDtypeStruct((bsz, s, a), _BF16),
            jax.ShapeDtypeStruct((bsz, nb, a, MOBA_BLOCK), _BF16),
            jax.ShapeDtypeStruct((bsz, s, d), _BF16),
            jax.ShapeDtypeStruct((bsz, s, d), _BF16),
        ],
        scratch_shapes=[pltpu.VMEM((SUBLANES, CONV_WIDTH), _F32)],
        compiler_params=pltpu.CompilerParams(
            dimension_semantics=("arbitrary", "arbitrary"), vmem_limit_bytes=VMEM_LIMIT_BYTES),
        name="inproj",
    )(x, pos3, inv_freq, wqkvt, wrest, bg, cw, wcnv)


def _moba_kernel(qt_ref, k_ref, vt_ref, o_ref, kmean_ref, mrow_ref, qm_ref, s_ref, p_ref, acc_ref,
                 m_ref, mnew_ref, shift_ref, alpha_ref):
    i = pl.program_id(2)
    nb = vt_ref.shape[1]
    blk = MOBA_BLOCK
    neg_inf = -jnp.inf

    @pl.when(i == 0)
    def _():
        def mean_body(j, _):
            kb = k_ref[0, pl.ds(pl.multiple_of(j * blk, blk), blk), :].astype(_F32)
            kmean_ref[pl.ds(j, 1), :] = jnp.sum(kb, axis=0, keepdims=True) / blk
            return 0
        lax.fori_loop(0, nb, mean_body, 0)

    n_heads = qt_ref.shape[1] // HEAD_DIM
    gw = GROUP_WIDTH
    grow = lax.broadcasted_iota(jnp.int32, (gw, blk), 0)
    for h in range(n_heads):
        g, hh = divmod(h, HEADS_PER_GROUP)
        qg = qt_ref[0, g * gw:(g + 1) * gw, :]
        qm_ref[h] = jnp.where((grow >= hh * HEAD_DIM) & (grow < (hh + 1) * HEAD_DIM), qg, jnp.zeros_like(qg))
    gsl = lambda h: slice((h // HEADS_PER_GROUP) * gw, (h // HEADS_PER_GROUP + 1) * gw)

    kmean = kmean_ref[...].astype(_BF16)
    brow = lax.broadcasted_iota(jnp.int32, (nb, blk), 0)
    for h in range(n_heads):
        sc = jnp.dot(kmean[:, gsl(h)], qm_ref[h], preferred_element_type=_F32)
        sc = jnp.where(brow < i, sc, neg_inf)
        sel = jnp.zeros((nb, blk), dtype=jnp.bool_)
        for _ in range(MOBA_TOPK):
            best = jnp.max(sc, axis=0, keepdims=True)
            idx = jnp.min(jnp.where(sc == best, brow, nb), axis=0, keepdims=True)
            pick = brow == idx
            sel = sel | pick
            sc = jnp.where(pick, neg_inf, sc)
        sel = sel & (brow < i)
        mrow_ref[h] = jnp.where(sel, 0.0, jnp.inf).astype(_F32)

    last = n_heads - 1
    for h in range(n_heads):
        m_ref[h] = jnp.full((1, blk), SOFTMAX_M_INIT, _F32)
        acc_ref[h] = jnp.zeros((ACC_ROWS, blk), _F32)
    p_ref[last] = jnp.zeros((blk, blk), _BF16)
    alpha_ref[last] = jnp.ones((1, blk), _F32)

    kpos = lax.broadcasted_iota(jnp.int32, (blk, blk), 0)
    qpos = lax.broadcasted_iota(jnp.int32, (blk, blk), 1)
    ones_rows = jnp.ones((ACC_ROWS - HEAD_DIM, blk), _BF16)

    def scores(j, h):
        kj = k_ref[0, pl.ds(pl.multiple_of(j * blk, blk), blk), gsl(h)]
        s_ref[h] = jnp.dot(kj, qm_ref[h], preferred_element_type=_F32)

    def load_scores(h, own):
        s = s_ref[h]
        return jnp.where(kpos <= qpos, s, neg_inf) if own else s

    def stats(j, h, own):
        cmax = jnp.max(load_scores(h, own), axis=0, keepdims=True)
        m_old = m_ref[h]
        if own:
            m_new = jnp.maximum(m_old, cmax)
            shift_ref[h] = m_new
        else:
            off = mrow_ref[h, pl.ds(j, 1), :]
            m_new = jnp.maximum(m_old, cmax - off)
            shift_ref[h] = m_new + off
        mnew_ref[h] = m_new
        alpha_ref[h] = jnp.exp2(m_old - m_new)

    def expo(h, own):
        p_ref[h] = jnp.exp2(load_scores(h, own) - shift_ref[h]).astype(_BF16)
        m_ref[h] = mnew_ref[h]

    def pv_unit(j, h):
        vt1 = jnp.concatenate([vt_ref[0, j, h * HEAD_DIM:(h + 1) * HEAD_DIM, :], ones_rows], axis=0)
        pv = jnp.dot(vt1, p_ref[h], preferred_element_type=_F32)
        acc_ref[h] = alpha_ref[h] * acc_ref[h] + pv

    def process_block(j, own, j_next, j_prev):
        for h in range(n_heads):
            ahead = h + SCORE_LOOKAHEAD
            if ahead < n_heads:
                scores(j, ahead)
            else:
                scores(j_next, ahead - n_heads)
            if h + 1 < n_heads:
                stats(j, h + 1, own)
            else:
                stats(j_next, 0, False)
            expo(h, own)
            if h == 0:
                pv_unit(j_prev, last)
            else:
                pv_unit(j, h - 1)

    for h in range(SCORE_LOOKAHEAD):
        scores(i, h)
    stats(i, 0, True)
    process_block(i, True, 0, i)

    def past_block(j):
        process_block(j, False, jnp.minimum(j + 1, nb - 1), jnp.where(j == 0, i, j - 1))

    def blocks_body(per_trip):
        def body(t, carry):
            for r in range(per_trip):
                past_block(per_trip * t + r)
            return carry
        return body

    n_quads = lax.shift_right_logical(i, 2)
    n_pairs = lax.shift_right_logical(i, 1)
    lax.fori_loop(0, n_quads, blocks_body(4), 0)
    lax.fori_loop(2 * n_quads, n_pairs, blocks_body(2), 0)
    lax.fori_loop(2 * n_pairs, i, blocks_body(1), 0)
    pv_unit(jnp.where(i == 0, i, i - 1), last)

    ot = jnp.concatenate(
        [acc_ref[h, 0:HEAD_DIM, :] / acc_ref[h, HEAD_DIM:HEAD_DIM + 1, :] for h in range(n_heads)], axis=0)
    o_ref[0] = ot.T.astype(_BF16)


def _moba(qt, k, vt):
    bsz, a, s = qt.shape
    nb = s // MOBA_BLOCK
    gw = GROUP_WIDTH * MOBA_GROUPS_PER_STEP
    n_heads = HEADS_PER_GROUP * MOBA_GROUPS_PER_STEP
    assert SCORE_LOOKAHEAD < n_heads
    return pl.pallas_call(
        _moba_kernel,
        grid=(bsz, N_GROUPS // MOBA_GROUPS_PER_STEP, nb),
        in_specs=[
            pl.BlockSpec((1, gw, MOBA_BLOCK), lambda b, g, i: (b, g, i)),
            pl.BlockSpec((1, s, gw), lambda b, g, i: (b, 0, g)),
            pl.BlockSpec((1, nb, gw, MOBA_BLOCK), lambda b, g, i: (b, 0, g, 0)),
        ],
        out_specs=pl.BlockSpec((1, MOBA_BLOCK, gw), lambda b, g, i: (b, i, g)),
        out_shape=jax.ShapeDtypeStruct((bsz, s, a), _BF16),
        scratch_shapes=[
            pltpu.VMEM((nb, gw), _F32),
            pltpu.VMEM((n_heads, nb, MOBA_BLOCK), _F32),
            pltpu.VMEM((n_heads, GROUP_WIDTH, MOBA_BLOCK), _BF16),
            pltpu.VMEM((n_heads, MOBA_BLOCK, MOBA_BLOCK), _F32),
            pltpu.VMEM((n_heads, MOBA_BLOCK, MOBA_BLOCK), _BF16),
            pltpu.VMEM((n_heads, ACC_ROWS, MOBA_BLOCK), _F32),
            pltpu.VMEM((n_heads, 1, MOBA_BLOCK), _F32),
            pltpu.VMEM((n_heads, 1, MOBA_BLOCK), _F32),
            pltpu.VMEM((n_heads, 1, MOBA_BLOCK), _F32),
            pltpu.VMEM((n_heads, 1, MOBA_BLOCK), _F32),
        ],
        compiler_params=pltpu.CompilerParams(
            dimension_semantics=("arbitrary", "arbitrary", "arbitrary"), vmem_limit_bytes=VMEM_LIMIT_BYTES),
        name="moba",
    )(qt, k, vt)


def _post_kernel(x_ref, attn_ref, gatt_ref, cnvg_ref, watt_ref, wo_ref, g1_ref, b1_ref,
                 wupg_ref, wupv_ref, cwg_ref, cwv_ref, wdown_ref, g2_ref, b2_ref,
                 o_ref, x1_ref, x1b_ref, u_ref, carryg_ref, carryv_ref, hmid_ref):
    tile = x_ref.shape[1]
    half = tile // 2
    halves = (slice(0, half), slice(half, tile))
    n_chunks = wupg_ref.shape[0]

    @pl.when(pl.program_id(1) == 0)
    def _():
        carryg_ref[...] = jnp.zeros_like(carryg_ref)
        carryv_ref[...] = jnp.zeros_like(carryv_ref)

    def attn_out(rs):
        return jnp.dot(attn_ref[0, rs, :], watt_ref[...], preferred_element_type=_F32)

    def gate_mix(rs, y_att):
        t = gatt_ref[0, rs, :].astype(_F32) * y_att + cnvg_ref[0, rs, :].astype(_F32)
        x1b_ref[rs, :] = t.astype(_BF16)

    def mix_proj(rs):
        return jnp.dot(x1b_ref[rs, :], wo_ref[...], preferred_element_type=_F32)

    def norm1(rs, mix):
        x1 = _layer_norm(DEEPNORM_ALPHA * x_ref[0, rs, :] + mix, g1_ref[...], b1_ref[...])
        x1_ref[rs, :] = x1
        x1b_ref[rs, :] = x1.astype(_BF16)

    units = [(rs, c) for rs in halves for c in range(n_chunks)]

    def up_proj(n):
        rs, c = units[n]
        u_ref[n % 2, 0] = jnp.dot(x1b_ref[rs, :], wupg_ref[c], preferred_element_type=_F32)
        u_ref[n % 2, 1] = jnp.dot(x1b_ref[rs, :], wupv_ref[c], preferred_element_type=_F32)

    def conv(u, cw_ref, carry_ref, c):
        out = _causal_conv3(u, carry_ref[c], cw_ref[c])
        carry_ref[c] = u[half - SUBLANES:half]
        return out

    def gated_conv(n):
        rs, c = units[n]
        cg = conv(u_ref[n % 2, 0], cwg_ref, carryg_ref, c)
        cv = conv(u_ref[n % 2, 1], cwv_ref, carryv_ref, c)
        hmid_ref[c, rs, :] = _silu_times(cg, cv).astype(_BF16)

    def down_proj(rs):
        f = jnp.dot(hmid_ref[0, rs, :], wdown_ref[0], preferred_element_type=_F32)
        for c in range(1, n_chunks):
            f = f + jnp.dot(hmid_ref[c, rs, :], wdown_ref[c], preferred_element_type=_F32)
        return f

    def norm2(rs, f):
        o_ref[0, rs, :] = _layer_norm(DEEPNORM_ALPHA * x1_ref[rs, :] + f, g2_ref[...], b2_ref[...])

    ha, hb = halves
    ya = attn_out(ha)
    yb = attn_out(hb)
    gate_mix(ha, ya)
    mix_a = mix_proj(ha)
    gate_mix(hb, yb)
    mix_b = mix_proj(hb)
    norm1(ha, mix_a)
    up_proj(0)
    norm1(hb, mix_b)
    for n in range(len(units)):
        if n + 1 < len(units):
            up_proj(n + 1)
        gated_conv(n)
    fa = down_proj(ha)
    fb = down_proj(hb)
    norm2(ha, fa)
    norm2(hb, fb)


def _post(x, attn, gatt, cnvg, watt, wo, g1, b1, wupg, wupv, cwg, cwv, wdown, g2, b2):
    bsz, s, d = x.shape
    tile = TOKEN_TILE
    n_chunks = wupg.shape[0]
    tok = lambda w: pl.BlockSpec((1, tile, w), lambda b, t: (b, t, 0))
    weights = (watt, wo, g1, b1, wupg, wupv, cwg, cwv, wdown, g2, b2)
    return pl.pallas_call(
        _post_kernel,
        grid=(bsz, s // tile),
        in_specs=[tok(d), tok(ATTN_WIDTH), tok(d), tok(d)] + [_resident(w.shape) for w in weights],
        out_specs=tok(d),
        out_shape=jax.ShapeDtypeStruct((bsz, s, d), x.dtype),
        scratch_shapes=[
            pltpu.VMEM((tile, d), _F32),
            pltpu.VMEM((tile, d), _BF16),
            pltpu.VMEM((2, 2, tile // 2, FF_CHUNK), _F32),
            pltpu.VMEM((n_chunks, SUBLANES, FF_CHUNK), _F32),
            pltpu.VMEM((n_chunks, SUBLANES, FF_CHUNK), _F32),
            pltpu.VMEM((n_chunks, tile, FF_CHUNK), _BF16),
        ],
        compiler_params=pltpu.CompilerParams(
            dimension_semantics=("arbitrary", "arbitrary"), vmem_limit_bytes=VMEM_LIMIT_BYTES),
        name="post",
    )(x, attn, gatt, cnvg, *weights)


def _chunk_cols(w, n_chunks):
    return w.reshape(w.shape[0], n_chunks, FF_CHUNK).transpose(1, 0, 2)


def kernel(x, positions, w_in, b_gate, w_attn_out, conv_w_mix, w_conv_out, w_o,
           ln1_g, ln1_b, w_up, conv_w_ffn, w_down, ln2_g, ln2_b):
    bsz, s, d = x.shape
    assert s % TOKEN_TILE == 0 and TOKEN_TILE % MOBA_BLOCK == 0 and D_FF % FF_CHUNK == 0
    a = ATTN_WIDTH
    n_chunks = D_FF // FF_CHUNK
    inv_freq = (ROPE_THETA ** (-jnp.arange(0, ROT_DIM, 2, dtype=_F32) / ROT_DIM)).reshape(ROT_HALF, 1)
    pos3 = positions.reshape(bsz, 1, s)
    for l in range(DEPTH):
        wqkvt = w_in[l, :, :3 * a].T.astype(_BF16)
        wrest = w_in[l, :, 3 * a:].astype(_BF16)
        qt, k, vt, gatt, cnvg = _inproj(
            x, pos3, inv_freq, wqkvt, wrest, b_gate[l].reshape(1, -1), conv_w_mix[l],
            w_conv_out[l].astype(_BF16))
        attn = _moba(qt, k, vt)
        row = lambda v: v.reshape(1, -1)
        x = _post(
            x, attn, gatt, cnvg, w_attn_out[l].astype(_BF16), w_o[l].astype(_BF16),
            row(ln1_g[l]), row(ln1_b[l]),
            _chunk_cols(w_up[l, :, :D_FF].astype(_BF16), n_chunks),
            _chunk_cols(w_up[l, :, D_FF:].astype(_BF16), n_chunks),
            _chunk_cols(conv_w_ffn[l, :, :D_FF], n_chunks),
            _chunk_cols(conv_w_ffn[l, :, D_FF:], n_chunks),
            w_down[l].astype(_BF16).reshape(n_chunks, FF_CHUNK, d),
            row(ln2_g[l]), row(ln2_b[l]))
    return x
```

```python
import functools
import math

import jax
import jax.numpy as jnp
from jax import lax
from jax.experimental import pallas as pl
from jax.experimental.pallas import tpu as pltpu

N_HEADS = 8
HEAD_DIM = 64
ATTN_WIDTH = N_HEADS * HEAD_DIM
MOBA_BLOCK = 256
MOBA_TOPK = 3
ROPE_THETA = 500000.0
ROT_DIM = HEAD_DIM // 4
ROT_HALF = ROT_DIM // 2
CONV_WIDTH = 512
CONV_K = 3
D_FF = 2816
DEPTH = 1
DEEPNORM_ALPHA = (2.0 * DEPTH) ** 0.25
LN_EPS = 1e-5

Q_SCALE = (1.0 / math.sqrt(HEAD_DIM)) * math.log2(math.e)

HEADS_PER_GROUP = 2
GROUP_WIDTH = HEADS_PER_GROUP * HEAD_DIM
N_GROUPS = N_HEADS // HEADS_PER_GROUP
MOBA_GROUPS_PER_STEP = 4
SCORE_LOOKAHEAD = 4
SOFTMAX_M_INIT = -1e30
BF16_SUBLANES = 16
ACC_ROWS = HEAD_DIM + BF16_SUBLANES
SUBLANES = 8
TOKEN_TILE = 512
FF_CHUNK = 256
VMEM_LIMIT_BYTES = 56 * 1024 * 1024

_NT_DIMS = (((1,), (1,)), ((), ()))
_F32 = jnp.float32
_BF16 = jnp.bfloat16


def _resident(shape):
    nd = len(shape)
    return pl.BlockSpec(shape, lambda *_: (0,) * nd, pipeline_mode=pl.Buffered(1))


def _causal_conv3(cur, carry, w):
    rows, width = cur.shape
    groups = rows // SUBLANES
    is_last = lax.broadcasted_iota(jnp.int32, (groups, SUBLANES, width), 1) == SUBLANES - 1

    def shift1(a, a_before):
        a3 = a.reshape(groups, SUBLANES, width)
        prev3 = jnp.concatenate([a_before[None], a3[:-1]], axis=0)
        return pltpu.roll(jnp.where(is_last, prev3, a3), 1, axis=1).reshape(rows, width)

    inner = w[1:2] * cur + shift1(w[0:1] * cur, w[0:1] * carry)
    inner_before = w[1:2] * carry + w[0:1] * pltpu.roll(carry, 1, axis=0)
    return w[2:3] * cur + shift1(inner, inner_before)


def _tree_reduce(fn, xs):
    while len(xs) > 1:
        xs = [fn(xs[k], xs[k + 1]) for k in range(0, len(xs) - 1, 2)] + (xs[-1:] if len(xs) % 2 else [])
    return xs[0]


def _sigmoid(x):
    return 0.5 * jnp.tanh(0.5 * x) + 0.5


def _silu_times(g, v):
    h = 0.5 * g
    return h * (jnp.tanh(h) + 1.0) * v


def _layer_norm(z, g, b):
    mu = jnp.mean(z, axis=-1, keepdims=True)
    zc = z - mu
    var = jnp.mean(zc * zc, axis=-1, keepdims=True)
    return zc * lax.rsqrt(var + LN_EPS) * g + b


def _inproj_kernel(x_ref, pos_ref, invf_ref, wqkvt_ref, wrest_ref, bg_ref, cw_ref, wcnv_ref,
                   qt_ref, k_ref, vt_ref, gatt_ref, cnvg_ref, carry_ref):
    tile = x_ref.shape[1]
    d_model = x_ref.shape[2]
    c = CONV_WIDTH
    xb = x_ref[0].astype(_BF16)

    gates_pre = jnp.dot(xb, wrest_ref[:, 3 * c:], preferred_element_type=_F32)
    qkvt = lax.dot_general(wqkvt_ref[...], xb, _NT_DIMS, preferred_element_type=_F32)
    gates = _sigmoid(gates_pre + bg_ref[...])
    gatt_ref[0] = gates[:, :d_model].astype(_BF16)
    g_cnv = gates[:, d_model:]

    rest = jnp.dot(xb, wrest_ref[:, 0:3 * c], preferred_element_type=_F32)
    ang = invf_ref[...] * pos_ref[0].astype(_F32)
    cos, sin = jnp.cos(ang), jnp.sin(ang)

    def rope_t(t):
        parts = []
        for h in range(N_HEADS):
            r = h * HEAD_DIM
            x1 = t[r:r + ROT_HALF]
            x2 = t[r + ROT_HALF:r + ROT_DIM]
            parts += [x1 * cos - x2 * sin, x2 * cos + x1 * sin, t[r + ROT_DIM:r + HEAD_DIM]]
        return jnp.concatenate(parts, axis=0)

    a = ATTN_WIDTH
    qt_ref[0] = (rope_t(qkvt[0:a]) * Q_SCALE).astype(_BF16)
    k_ref[0] = rope_t(qkvt[a:2 * a]).T.astype(_BF16)
    vt = qkvt[2 * a:3 * a].astype(_BF16)
    for j in range(tile // MOBA_BLOCK):
        vt_ref[0, j] = vt[:, j * MOBA_BLOCK:(j + 1) * MOBA_BLOCK]

    gate_b, gate_c, h_in = rest[:, 0:c], rest[:, c:2 * c], rest[:, 2 * c:3 * c]
    ch = gate_c * h_in

    @pl.when(pl.program_id(1) == 0)
    def _():
        carry_ref[...] = jnp.zeros_like(carry_ref)

    conv = _causal_conv3(ch, carry_ref[...], cw_ref[...])
    carry_ref[...] = ch[tile - SUBLANES:tile]
    y_cnv = jnp.dot((gate_b * conv).astype(_BF16), wcnv_ref[...], preferred_element_type=_F32)
    cnvg_ref[0] = (g_cnv * y_cnv).astype(_BF16)


def _inproj(x, pos3, inv_freq, wqkvt, wrest, bg, cw, wcnv):
    bsz, s, d = x.shape
    tile = TOKEN_TILE
    nb = s // MOBA_BLOCK
    a = ATTN_WIDTH
    tok = lambda w: pl.BlockSpec((1, tile, w), lambda b, t: (b, t, 0))
    return pl.pallas_call(
        _inproj_kernel,
        grid=(bsz, s // tile),
        in_specs=[
            tok(d),
            pl.BlockSpec((1, 1, tile), lambda b, t: (b, 0, t)),
            _resident(inv_freq.shape),
            _resident(wqkvt.shape),
            _resident(wrest.shape),
            _resident(bg.shape),
            _resident(cw.shape),
            _resident(wcnv.shape),
        ],
        out_specs=[
            pl.BlockSpec((1, a, tile), lambda b, t: (b, 0, t)),
            tok(a),
            pl.BlockSpec((1, tile // MOBA_BLOCK, a, MOBA_BLOCK), lambda b, t: (b, t, 0, 0)),
            tok(d),
            tok(d),
        ],
        out_shape=[
            jax.ShapeDtypeStruct((bsz, a, s), _BF16),
            jax.ShapeDtypeStruct((bsz, s, a), _BF16),
            jax.ShapeDtypeStruct((bsz, nb, a, MOBA_BLOCK), _BF16),
            jax.ShapeDtypeStruct((bsz, s, d), _BF16),
            jax.ShapeDtypeStruct((bsz, s, d), _BF16),
        ],
        scratch_shapes=[pltpu.VMEM((SUBLANES, CONV_WIDTH), _F32)],
        compiler_params=pltpu.CompilerParams(
            dimension_semantics=("arbitrary", "arbitrary"), vmem_limit_bytes=VMEM_LIMIT_BYTES),
        name="inproj",
    )(x, pos3, inv_freq, wqkvt, wrest, bg, cw, wcnv)


def _moba_kernel(qt_ref, k_ref, vt_ref, o_ref, kmean_ref, mrow_ref, qm_ref, s_ref, p_ref, acc_ref,
                 m_ref, cmax_ref, alpha_ref):
    i = pl.program_id(2)
    nb = vt_ref.shape[1]
    blk = MOBA_BLOCK
    neg_inf = -jnp.inf

    n_heads = qt_ref.shape[1] // HEAD_DIM
    width = qt_ref.shape[1]
    gw = GROUP_WIDTH

    @pl.when(i == 0)
    def _():
        chan = lax.broadcasted_iota(jnp.int32, (n_heads, width), 1)
        head0 = lax.broadcasted_iota(jnp.int32, (n_heads, width), 0) * HEAD_DIM
        in_head = (chan >= head0) & (chan < head0 + HEAD_DIM)

        def mean_body(j, _):
            kb = k_ref[0, pl.ds(pl.multiple_of(j * blk, blk), blk), :].astype(_F32)
            km = jnp.sum(kb, axis=0, keepdims=True) / blk
            kmean_ref[pl.ds(pl.multiple_of(j * n_heads, n_heads), n_heads), :] = jnp.where(in_head, km, 0.0)
            return 0
        lax.fori_loop(0, nb, mean_body, 0)

    grow = lax.broadcasted_iota(jnp.int32, (gw, blk), 0)
    for h in range(n_heads):
        g, hh = divmod(h, HEADS_PER_GROUP)
        qg = qt_ref[0, g * gw:(g + 1) * gw, :]
        qm_ref[h] = jnp.where((grow >= hh * HEAD_DIM) & (grow < (hh + 1) * HEAD_DIM), qg, jnp.zeros_like(qg))
    gsl = lambda h: slice((h // HEADS_PER_GROUP) * gw, (h // HEADS_PER_GROUP + 1) * gw)

    sc_all = jnp.dot(kmean_ref[...].astype(_BF16), qt_ref[0], preferred_element_type=_F32)
    scs = [jnp.where(j < i, sc_all[j * n_heads:(j + 1) * n_heads], neg_inf) for j in range(nb)]
    offs = [jnp.full((n_heads, blk), jnp.inf, _F32)] * nb
    for _ in range(MOBA_TOPK):
        best = _tree_reduce(jnp.maximum, scs)
        idx = _tree_reduce(jnp.minimum, [jnp.where(scs[j] == best, j, nb) for j in range(nb)])
        picks = [idx == j for j in range(nb)]
        offs = [jnp.where(picks[j], 0.0, offs[j]) for j in range(nb)]
        scs = [jnp.where(picks[j], neg_inf, scs[j]) for j in range(nb)]
    for j in range(nb):
        mrow_ref[j] = jnp.where(j < i, offs[j], jnp.inf)

    last = n_heads - 1
    for h in range(n_heads):
        m_ref[h] = jnp.full((1, blk), SOFTMAX_M_INIT, _F32)
        acc_ref[h] = jnp.zeros((ACC_ROWS, blk), _F32)
    p_ref[last] = jnp.zeros((blk, blk), _BF16)
    alpha_ref[last] = jnp.ones((1, blk), _F32)

    kpos = lax.broadcasted_iota(jnp.int32, (blk, blk), 0)
    qpos = lax.broadcasted_iota(jnp.int32, (blk, blk), 1)
    ones_rows = jnp.ones((ACC_ROWS - HEAD_DIM, blk), _BF16)

    def scores(j, h, own):
        kj = k_ref[0, pl.ds(pl.multiple_of(j * blk, blk), blk), gsl(h)]
        s = jnp.dot(kj, qm_ref[h], preferred_element_type=_F32)
        if own:
            s = jnp.where(kpos <= qpos, s, neg_inf)
        s_ref[h] = s
        cmax_ref[h] = jnp.max(s, axis=0, keepdims=True)

    def expo(j, h, own):
        m_old = m_ref[h]
        if own:
            m_new = jnp.maximum(m_old, cmax_ref[h])
            shift = m_new
        else:
            off = mrow_ref[j, pl.ds(h, 1), :]
            m_new = jnp.maximum(m_old, cmax_ref[h] - off)
            shift = m_new + off
        m_ref[h] = m_new
        alpha_ref[h] = jnp.exp2(m_old - m_new)
        p_ref[h] = jnp.exp2(s_ref[h] - shift).astype(_BF16)

    def pv_unit(j, h):
        vt1 = jnp.concatenate([vt_ref[0, j, h * HEAD_DIM:(h + 1) * HEAD_DIM, :], ones_rows], axis=0)
        pv = jnp.dot(vt1, p_ref[h], preferred_element_type=_F32)
        acc_ref[h] = alpha_ref[h] * acc_ref[h] + pv

    def process_block(j, own, j_next, j_prev):
        for h in range(n_heads):
            ahead = h + SCORE_LOOKAHEAD
            if ahead < n_heads:
                scores(j, ahead, own)
            else:
                scores(j_next, ahead - n_heads, False)
            expo(j, h, own)
            if h == 0:
                pv_unit(j_prev, last)
            else:
                pv_unit(j, h - 1)

    for h in range(SCORE_LOOKAHEAD):
        scores(i, h, True)
    process_block(i, True, 0, i)

    def past_block(j):
        process_block(j, False, jnp.minimum(j + 1, nb - 1), jnp.where(j == 0, i, j - 1))

    def blocks_body(per_trip):
        def body(t, carry):
            for r in range(per_trip):
                past_block(per_trip * t + r)
            return carry
        return body

    n_octs = lax.shift_right_logical(i, 3)
    n_quads = lax.shift_right_logical(i, 2)
    n_pairs = lax.shift_right_logical(i, 1)
    lax.fori_loop(0, n_octs, blocks_body(8), 0)
    lax.fori_loop(2 * n_octs, n_quads, blocks_body(4), 0)
    lax.fori_loop(2 * n_quads, n_pairs, blocks_body(2), 0)
    lax.fori_loop(2 * n_pairs, i, blocks_body(1), 0)
    pv_unit(jnp.where(i == 0, i, i - 1), last)

    ot = jnp.concatenate(
        [acc_ref[h, 0:HEAD_DIM, :] / acc_ref[h, HEAD_DIM:HEAD_DIM + 1, :] for h in range(n_heads)], axis=0)
    o_ref[0] = ot.T.astype(_BF16)


def _moba(qt, k, vt):
    bsz, a, s = qt.shape
    nb = s // MOBA_BLOCK
    gw = GROUP_WIDTH * MOBA_GROUPS_PER_STEP
    n_heads = HEADS_PER_GROUP * MOBA_GROUPS_PER_STEP
    assert SCORE_LOOKAHEAD < n_heads and n_heads == SUBLANES
    return pl.pallas_call(
        _moba_kernel,
        grid=(bsz, N_GROUPS // MOBA_GROUPS_PER_STEP, nb),
        in_specs=[
            pl.BlockSpec((1, gw, MOBA_BLOCK), lambda b, g, i: (b, g, i)),
            pl.BlockSpec((1, s, gw), lambda b, g, i: (b, 0, g)),
            pl.BlockSpec((1, nb, gw, MOBA_BLOCK), lambda b, g, i: (b, 0, g, 0)),
        ],
        out_specs=pl.BlockSpec((1, MOBA_BLOCK, gw), lambda b, g, i: (b, i, g)),
        out_shape=jax.ShapeDtypeStruct((bsz, s, a), _BF16),
        scratch_shapes=[
            pltpu.VMEM((nb * n_heads, gw), _F32),
            pltpu.VMEM((nb, n_heads, MOBA_BLOCK), _F32),
            pltpu.VMEM((n_heads, GROUP_WIDTH, MOBA_BLOCK), _BF16),
            pltpu.VMEM((n_heads, MOBA_BLOCK, MOBA_BLOCK), _F32),
            pltpu.VMEM((n_heads, MOBA_BLOCK, MOBA_BLOCK), _BF16),
            pltpu.VMEM((n_heads, ACC_ROWS, MOBA_BLOCK), _F32),
            pltpu.VMEM((n_heads, 1, MOBA_BLOCK), _F32),
            pltpu.VMEM((n_heads, 1, MOBA_BLOCK), _F32),
            pltpu.VMEM((n_heads, 1, MOBA_BLOCK), _F32),
        ],
        compiler_params=pltpu.CompilerParams(
            dimension_semantics=("arbitrary", "arbitrary", "arbitrary"), vmem_limit_bytes=VMEM_LIMIT_BYTES),
        name="moba",
    )(qt, k, vt)


def _post_kernel(x_ref, attn_ref, gatt_ref, cnvg_ref, watt_ref, wo_ref, g1_ref, b1_ref,
                 wup_ref, cw_ref, wdown_ref, g2_ref, b2_ref,
                 o_ref, x1_ref, x1b_ref, u_ref, carryg_ref, carryv_ref, hmid_ref):
    tile = x_ref.shape[1]
    half = tile // 2
    halves = (slice(0, half), slice(half, tile))
    n_chunks = D_FF // FF_CHUNK
    gate_cols = lambda c: slice(c * FF_CHUNK, (c + 1) * FF_CHUNK)
    val_cols = lambda c: slice(D_FF + c * FF_CHUNK, D_FF + (c + 1) * FF_CHUNK)

    @pl.when(pl.program_id(1) == 0)
    def _():
        carryg_ref[...] = jnp.zeros_like(carryg_ref)
        carryv_ref[...] = jnp.zeros_like(carryv_ref)

    def attn_out(rs):
        return jnp.dot(attn_ref[0, rs, :], watt_ref[...], preferred_element_type=_F32)

    def gate_mix(rs, y_att):
        t = gatt_ref[0, rs, :].astype(_F32) * y_att + cnvg_ref[0, rs, :].astype(_F32)
        x1b_ref[rs, :] = t.astype(_BF16)

    def mix_proj(rs):
        return jnp.dot(x1b_ref[rs, :], wo_ref[...], preferred_element_type=_F32)

    def norm1(rs, mix):
        x1 = _layer_norm(DEEPNORM_ALPHA * x_ref[0, rs, :] + mix, g1_ref[...], b1_ref[...])
        x1_ref[rs, :] = x1
        x1b_ref[rs, :] = x1.astype(_BF16)

    units = [(rs, c) for rs in halves for c in range(n_chunks)]

    def up_proj(n):
        rs, c = units[n]
        u_ref[n % 2, 0] = jnp.dot(x1b_ref[rs, :], wup_ref[:, gate_cols(c)], preferred_element_type=_F32)
        u_ref[n % 2, 1] = jnp.dot(x1b_ref[rs, :], wup_ref[:, val_cols(c)], preferred_element_type=_F32)

    def conv(u, cols, carry_ref, c):
        out = _causal_conv3(u, carry_ref[c], cw_ref[:, cols])
        carry_ref[c] = u[half - SUBLANES:half]
        return out

    def gated_conv(n):
        rs, c = units[n]
        cg = conv(u_ref[n % 2, 0], gate_cols(c), carryg_ref, c)
        cv = conv(u_ref[n % 2, 1], val_cols(c), carryv_ref, c)
        hmid_ref[c, rs, :] = _silu_times(cg, cv).astype(_BF16)

    def down_proj(rs):
        f = jnp.dot(hmid_ref[0, rs, :], wdown_ref[gate_cols(0), :], preferred_element_type=_F32)
        for c in range(1, n_chunks):
            f = f + jnp.dot(hmid_ref[c, rs, :], wdown_ref[gate_cols(c), :], preferred_element_type=_F32)
        return f

    def norm2(rs, f):
        o_ref[0, rs, :] = _layer_norm(DEEPNORM_ALPHA * x1_ref[rs, :] + f, g2_ref[...], b2_ref[...])

    ha, hb = halves
    ya = attn_out(ha)
    yb = attn_out(hb)
    gate_mix(ha, ya)
    mix_a = mix_proj(ha)
    gate_mix(hb, yb)
    mix_b = mix_proj(hb)
    norm1(ha, mix_a)
    up_proj(0)
    norm1(hb, mix_b)
    for n in range(len(units)):
        if n + 1 < len(units):
            up_proj(n + 1)
        gated_conv(n)
    fa = down_proj(ha)
    fb = down_proj(hb)
    norm2(ha, fa)
    norm2(hb, fb)


def _post(x, attn, gatt, cnvg, watt, wo, g1, b1, wup, cw, wdown, g2, b2):
    bsz, s, d = x.shape
    tile = TOKEN_TILE
    n_chunks = D_FF // FF_CHUNK
    tok = lambda w: pl.BlockSpec((1, tile, w), lambda b, t: (b, t, 0))
    weights = (watt, wo, g1, b1, wup, cw, wdown, g2, b2)
    return pl.pallas_call(
        _post_kernel,
        grid=(bsz, s // tile),
        in_specs=[tok(d), tok(ATTN_WIDTH), tok(d), tok(d)] + [_resident(w.shape) for w in weights],
        out_specs=tok(d),
        out_shape=jax.ShapeDtypeStruct((bsz, s, d), x.dtype),
        scratch_shapes=[
            pltpu.VMEM((tile, d), _F32),
            pltpu.VMEM((tile, d), _BF16),
            pltpu.VMEM((2, 2, tile // 2, FF_CHUNK), _F32),
            pltpu.VMEM((n_chunks, SUBLANES, FF_CHUNK), _F32),
            pltpu.VMEM((n_chunks, SUBLANES, FF_CHUNK), _F32),
            pltpu.VMEM((n_chunks, tile, FF_CHUNK), _BF16),
        ],
        compiler_params=pltpu.CompilerParams(
            dimension_semantics=("arbitrary", "arbitrary"), vmem_limit_bytes=VMEM_LIMIT_BYTES),
        name="post",
    )(x, attn, gatt, cnvg, *weights)


def kernel(x, positions, w_in, b_gate, w_attn_out, conv_w_mix, w_conv_out, w_o,
           ln1_g, ln1_b, w_up, conv_w_ffn, w_down, ln2_g, ln2_b):
    bsz, s, d = x.shape
    assert s % TOKEN_TILE == 0 and TOKEN_TILE % MOBA_BLOCK == 0 and D_FF % FF_CHUNK == 0
    a = ATTN_WIDTH
    inv_freq = (ROPE_THETA ** (-jnp.arange(0, ROT_DIM, 2, dtype=_F32) / ROT_DIM)).reshape(ROT_HALF, 1)
    pos3 = positions.reshape(bsz, 1, s)
    for l in range(DEPTH):
        wqkvt = w_in[l, :, :3 * a].T.astype(_BF16)
        wrest = w_in[l, :, 3 * a:].astype(_BF16)
        qt, k, vt, gatt, cnvg = _inproj(
            x, pos3, inv_freq, wqkvt, wrest, b_gate[l].reshape(1, -1), conv_w_mix[l],
            w_conv_out[l].astype(_BF16))
        attn = _moba(qt, k, vt)
        row = lambda v: v.reshape(1, -1)
        x = _post(
            x, attn, gatt, cnvg, w_attn_out[l].astype(_BF16), w_o[l].astype(_BF16),
            row(ln1_g[l]), row(ln1_b[l]),
            w_up[l].astype(_BF16), conv_w_ffn[l], w_down[l].astype(_BF16),
            row(ln2_g[l]), row(ln2_b[l]))
    return x
```

```python
import math

import jax
import jax.numpy as jnp
from jax import lax
from jax.experimental import pallas as pl
from jax.experimental.pallas import tpu as pltpu

N_HEADS = 8
HEAD_DIM = 64
ATTN_WIDTH = N_HEADS * HEAD_DIM
MOBA_BLOCK = 256
MOBA_TOPK = 3
ROPE_THETA = 500000.0
ROT_DIM = HEAD_DIM // 4
ROT_HALF = ROT_DIM // 2
CONV_WIDTH = 512
CONV_K = 3
D_FF = 2816
DEPTH = 1
DEEPNORM_ALPHA = (2.0 * DEPTH) ** 0.25
LN_EPS = 1e-5

Q_SCALE = (1.0 / math.sqrt(HEAD_DIM)) * math.log2(math.e)

HEADS_PER_GROUP = 2
GROUP_WIDTH = HEADS_PER_GROUP * HEAD_DIM
N_GROUPS = N_HEADS // HEADS_PER_GROUP
MOBA_GROUPS_PER_STEP = 4
SCORE_LOOKAHEAD = 4
BF16_SUBLANES = 16
ACC_ROWS = HEAD_DIM + BF16_SUBLANES
SUBLANES = 8
TOKEN_TILE = 512
FF_CHUNK = 256
VMEM_LIMIT_BYTES = 56 * 1024 * 1024

_NT_DIMS = (((1,), (1,)), ((), ()))
_TN_DIMS = (((0,), (0,)), ((), ()))
_F32 = jnp.float32
_BF16 = jnp.bfloat16


def _resident(shape):
    nd = len(shape)
    return pl.BlockSpec(shape, lambda *_: (0,) * nd, pipeline_mode=pl.Buffered(1))


def _causal_conv3(cur, carry, w):
    rows, width = cur.shape
    groups = rows // SUBLANES
    is_last = lax.broadcasted_iota(jnp.int32, (groups, SUBLANES, width), 1) == SUBLANES - 1

    def shift1(a, a_before):
        a3 = a.reshape(groups, SUBLANES, width)
        prev3 = jnp.concatenate([a_before[None], a3[:-1]], axis=0)
        return pltpu.roll(jnp.where(is_last, prev3, a3), 1, axis=1).reshape(rows, width)

    inner = w[1:2] * cur + shift1(w[0:1] * cur, w[0:1] * carry)
    inner_before = w[1:2] * carry + w[0:1] * pltpu.roll(carry, 1, axis=0)
    return w[2:3] * cur + shift1(inner, inner_before)


def _tree_reduce(fn, xs):
    while len(xs) > 1:
        xs = [fn(xs[k], xs[k + 1]) for k in range(0, len(xs) - 1, 2)] + (xs[-1:] if len(xs) % 2 else [])
    return xs[0]


def _sigmoid(x):
    return 0.5 * jnp.tanh(0.5 * x) + 0.5


def _silu_times(g, v):
    h = 0.5 * g
    return h * (jnp.tanh(h) + 1.0) * v


def _layer_norm(z, g, b):
    mu = jnp.mean(z, axis=-1, keepdims=True)
    zc = z - mu
    var = jnp.mean(zc * zc, axis=-1, keepdims=True)
    return zc * lax.rsqrt(var + LN_EPS) * g + b


def _inproj_kernel(x_ref, pos_ref, invf_ref, wqkvt_ref, wrest_ref, bg_ref, cw_ref, wcnv_ref,
                   qt_ref, k_ref, vt_ref, gatt_ref, cnvg_ref, carry_ref):
    tile = x_ref.shape[1]
    d_model = x_ref.shape[2]
    c = CONV_WIDTH
    xb = x_ref[0].astype(_BF16)

    gates_pre = jnp.dot(xb, wrest_ref[:, 3 * c:], preferred_element_type=_F32)
    qkvt = lax.dot_general(wqkvt_ref[...], xb, _NT_DIMS, preferred_element_type=_F32)
    gates = _sigmoid(gates_pre + bg_ref[...])
    gatt_ref[0] = gates[:, :d_model].astype(_BF16)
    g_cnv = gates[:, d_model:]

    rest = jnp.dot(xb, wrest_ref[:, 0:3 * c], preferred_element_type=_F32)
    ang = invf_ref[...] * pos_ref[0].astype(_F32)
    cos, sin = jnp.cos(ang), jnp.sin(ang)

    def rope_t(t):
        parts = []
        for h in range(N_HEADS):
            r = h * HEAD_DIM
            x1 = t[r:r + ROT_HALF]
            x2 = t[r + ROT_HALF:r + ROT_DIM]
            parts += [x1 * cos - x2 * sin, x2 * cos + x1 * sin, t[r + ROT_DIM:r + HEAD_DIM]]
        return jnp.concatenate(parts, axis=0)

    a = ATTN_WIDTH
    qt_ref[0] = (rope_t(qkvt[0:a]) * Q_SCALE).astype(_BF16)
    k_ref[0] = rope_t(qkvt[a:2 * a]).T.astype(_BF16)
    vt = qkvt[2 * a:3 * a].astype(_BF16)
    for j in range(tile // MOBA_BLOCK):
        vt_ref[0, j] = vt[:, j * MOBA_BLOCK:(j + 1) * MOBA_BLOCK]

    gate_b, gate_c, h_in = rest[:, 0:c], rest[:, c:2 * c], rest[:, 2 * c:3 * c]
    ch = gate_c * h_in

    @pl.when(pl.program_id(1) == 0)
    def _():
        carry_ref[...] = jnp.zeros_like(carry_ref)

    conv = _causal_conv3(ch, carry_ref[...], cw_ref[...])
    carry_ref[...] = ch[tile - SUBLANES:tile]
    y_cnv = jnp.dot((gate_b * conv).astype(_BF16), wcnv_ref[...], preferred_element_type=_F32)
    cnvg_ref[0] = (g_cnv * y_cnv).astype(_BF16)


def _inproj(x, pos3, inv_freq, wqkvt, wrest, bg, cw, wcnv):
    bsz, s, d = x.shape
    tile = TOKEN_TILE
    nb = s // MOBA_BLOCK
    a = ATTN_WIDTH
    tok = lambda w: pl.BlockSpec((1, tile, w), lambda b, t: (b, t, 0))
    return pl.pallas_call(
        _inproj_kernel,
        grid=(bsz, s // tile),
        in_specs=[
            tok(d),
            pl.BlockSpec((1, 1, tile), lambda b, t: (b, 0, t)),
            _resident(inv_freq.shape),
            _resident(wqkvt.shape),
            _resident(wrest.shape),
            _resident(bg.shape),
            _resident(cw.shape),
            _resident(wcnv.shape),
        ],
        out_specs=[
            pl.BlockSpec((1, a, tile), lambda b, t: (b, 0, t)),
            tok(a),
            pl.BlockSpec((1, tile // MOBA_BLOCK, a, MOBA_BLOCK), lambda b, t: (b, t, 0, 0)),
            tok(d),
            tok(d),
        ],
        out_shape=[
            jax.ShapeDtypeStruct((bsz, a, s), _BF16),
            jax.ShapeDtypeStruct((bsz, s, a), _BF16),
            jax.ShapeDtypeStruct((bsz, nb, a, MOBA_BLOCK), _BF16),
            jax.ShapeDtypeStruct((bsz, s, d), _BF16),
            jax.ShapeDtypeStruct((bsz, s, d), _BF16),
        ],
        scratch_shapes=[pltpu.VMEM((SUBLANES, CONV_WIDTH), _F32)],
        compiler_params=pltpu.CompilerParams(
            dimension_semantics=("arbitrary", "arbitrary"), vmem_limit_bytes=VMEM_LIMIT_BYTES),
        name="inproj",
    )(x, pos3, inv_freq, wqkvt, wrest, bg, cw, wcnv)


def _moba_kernel(qt_ref, k_ref, vt_ref, o_ref, kmean_ref, mrow_ref, qm_ref, s_ref, p_ref, acc_ref,
                 m_ref, cmax_ref, alpha_ref):
    i = pl.program_id(2)
    nb = vt_ref.shape[1]
    blk = MOBA_BLOCK
    neg_inf = -jnp.inf

    n_heads = qt_ref.shape[1] // HEAD_DIM
    width = qt_ref.shape[1]
    gw = GROUP_WIDTH

    @pl.when(i == 0)
    def _():
        chan = lax.broadcasted_iota(jnp.int32, (n_heads, width), 1)
        head0 = lax.broadcasted_iota(jnp.int32, (n_heads, width), 0) * HEAD_DIM
        in_head = (chan >= head0) & (chan < head0 + HEAD_DIM)

        def mean_body(j, _):
            kb = k_ref[0, pl.ds(pl.multiple_of(j * blk, blk), blk), :].astype(_F32)
            km = jnp.sum(kb, axis=0, keepdims=True) / blk
            kmean_ref[pl.ds(pl.multiple_of(j * n_heads, n_heads), n_heads), :] = jnp.where(in_head, km, 0.0)
            return 0
        lax.fori_loop(0, nb, mean_body, 0)

    grow = lax.broadcasted_iota(jnp.int32, (gw, blk), 0)
    for h in range(n_heads):
        g, hh = divmod(h, HEADS_PER_GROUP)
        qg = qt_ref[0, g * gw:(g + 1) * gw, :]
        qm_ref[h] = jnp.where((grow >= hh * HEAD_DIM) & (grow < (hh + 1) * HEAD_DIM), qg, jnp.zeros_like(qg))
    gsl = lambda h: slice((h // HEADS_PER_GROUP) * gw, (h // HEADS_PER_GROUP + 1) * gw)

    sc_all = jnp.dot(kmean_ref[...].astype(_BF16), qt_ref[0], preferred_element_type=_F32)
    scs = [jnp.where(j < i, sc_all[j * n_heads:(j + 1) * n_heads], neg_inf) for j in range(nb)]
    offs = [jnp.full((n_heads, blk), jnp.inf, _F32)] * nb
    for _ in range(MOBA_TOPK):
        best = _tree_reduce(jnp.maximum, scs)
        idx = _tree_reduce(jnp.minimum, [jnp.where(scs[j] == best, j, nb) for j in range(nb)])
        picks = [idx == j for j in range(nb)]
        offs = [jnp.where(picks[j], 0.0, offs[j]) for j in range(nb)]
        scs = [jnp.where(picks[j], neg_inf, scs[j]) for j in range(nb)]
    for j in range(nb):
        mrow_ref[j] = jnp.where(j < i, offs[j], jnp.inf)

    last = n_heads - 1
    for h in range(n_heads):
        acc_ref[h] = jnp.zeros((ACC_ROWS, blk), _F32)
    p_ref[last] = jnp.zeros((blk, blk), _BF16)
    alpha_ref[last] = jnp.ones((1, blk), _F32)

    kpos = lax.broadcasted_iota(jnp.int32, (blk, blk), 0)
    qpos = lax.broadcasted_iota(jnp.int32, (blk, blk), 1)
    ones_rows = jnp.ones((ACC_ROWS - HEAD_DIM, blk), _BF16)

    def scores(j, h, own):
        kj = k_ref[0, pl.ds(pl.multiple_of(j * blk, blk), blk), gsl(h)]
        s = jnp.dot(kj, qm_ref[h], preferred_element_type=_F32)
        if own:
            s = jnp.where(kpos <= qpos, s, neg_inf)
        s_ref[h] = s
        cmax_ref[h] = jnp.max(s, axis=0, keepdims=True)

    def expo(j, h, own):
        if own:
            m_new = cmax_ref[h]
            shift = m_new
            alpha_ref[h] = jnp.zeros((1, blk), _F32)
        else:
            m_old = m_ref[h]
            off = mrow_ref[j, pl.ds(h, 1), :]
            m_new = jnp.maximum(m_old, cmax_ref[h] - off)
            shift = m_new + off
            alpha_ref[h] = jnp.exp2(m_old - m_new)
        m_ref[h] = m_new
        p_ref[h] = jnp.exp2(s_ref[h] - shift).astype(_BF16)

    def pv_unit(j, h):
        vt1 = jnp.concatenate([vt_ref[0, j, h * HEAD_DIM:(h + 1) * HEAD_DIM, :], ones_rows], axis=0)
        pv = jnp.dot(vt1, p_ref[h], preferred_element_type=_F32)
        acc_ref[h] = alpha_ref[h] * acc_ref[h] + pv

    def process_block(j, own, j_next, j_prev):
        for h in range(n_heads):
            ahead = h + SCORE_LOOKAHEAD
            if ahead < n_heads:
                scores(j, ahead, own)
            else:
                scores(j_next, ahead - n_heads, False)
            expo(j, h, own)
            if h == 0:
                pv_unit(j_prev, last)
            else:
                pv_unit(j, h - 1)

    for h in range(SCORE_LOOKAHEAD):
        scores(i, h, True)
    process_block(i, True, 0, i)

    def past_block(j):
        process_block(j, False, jnp.minimum(j + 1, nb - 1), jnp.where(j == 0, i, j - 1))

    def blocks_body(per_trip):
        def body(t, carry):
            for r in range(per_trip):
                past_block(per_trip * t + r)
            return carry
        return body

    n_octs = lax.shift_right_logical(i, 3)
    n_quads = lax.shift_right_logical(i, 2)
    n_pairs = lax.shift_right_logical(i, 1)
    lax.fori_loop(0, n_octs, blocks_body(8), 0)
    lax.fori_loop(2 * n_octs, n_quads, blocks_body(4), 0)
    lax.fori_loop(2 * n_quads, n_pairs, blocks_body(2), 0)
    lax.fori_loop(2 * n_pairs, i, blocks_body(1), 0)
    pv_unit(jnp.where(i == 0, i, i - 1), last)

    for h in range(n_heads):
        o_ref[0, h * HEAD_DIM:(h + 1) * HEAD_DIM, :] = (
            acc_ref[h, 0:HEAD_DIM, :] / acc_ref[h, HEAD_DIM:HEAD_DIM + 1, :]).astype(_BF16)


def _moba(qt, k, vt):
    bsz, a, s = qt.shape
    nb = s // MOBA_BLOCK
    gw = GROUP_WIDTH * MOBA_GROUPS_PER_STEP
    n_heads = HEADS_PER_GROUP * MOBA_GROUPS_PER_STEP
    assert SCORE_LOOKAHEAD < n_heads and n_heads == SUBLANES
    return pl.pallas_call(
        _moba_kernel,
        grid=(bsz, N_GROUPS // MOBA_GROUPS_PER_STEP, nb),
        in_specs=[
            pl.BlockSpec((1, gw, MOBA_BLOCK), lambda b, g, i: (b, g, i)),
            pl.BlockSpec((1, s, gw), lambda b, g, i: (b, 0, g)),
            pl.BlockSpec((1, nb, gw, MOBA_BLOCK), lambda b, g, i: (b, 0, g, 0)),
        ],
        out_specs=pl.BlockSpec((1, gw, MOBA_BLOCK), lambda b, g, i: (b, g, i)),
        out_shape=jax.ShapeDtypeStruct((bsz, a, s), _BF16),
        scratch_shapes=[
            pltpu.VMEM((nb * n_heads, gw), _F32),
            pltpu.VMEM((nb, n_heads, MOBA_BLOCK), _F32),
            pltpu.VMEM((n_heads, GROUP_WIDTH, MOBA_BLOCK), _BF16),
            pltpu.VMEM((n_heads, MOBA_BLOCK, MOBA_BLOCK), _F32),
            pltpu.VMEM((n_heads, MOBA_BLOCK, MOBA_BLOCK), _BF16),
            pltpu.VMEM((n_heads, ACC_ROWS, MOBA_BLOCK), _F32),
            pltpu.VMEM((n_heads, 1, MOBA_BLOCK), _F32),
            pltpu.VMEM((n_heads, 1, MOBA_BLOCK), _F32),
            pltpu.VMEM((n_heads, 1, MOBA_BLOCK), _F32),
        ],
        compiler_params=pltpu.CompilerParams(
            dimension_semantics=("arbitrary", "arbitrary", "arbitrary"), vmem_limit_bytes=VMEM_LIMIT_BYTES),
        name="moba",
    )(qt, k, vt)


def _post_kernel(x_ref, attn_ref, gatt_ref, cnvg_ref, watt_ref, wo_ref, g1_ref, b1_ref,
                 wup_ref, cw_ref, wdown_ref, g2_ref, b2_ref,
                 o_ref, x1_ref, x1b_ref, u_ref, carryg_ref, carryv_ref, hmid_ref):
    tile = x_ref.shape[1]
    half = tile // 2
    halves = (slice(0, half), slice(half, tile))
    n_chunks = D_FF // FF_CHUNK
    gate_cols = lambda c: slice(c * FF_CHUNK, (c + 1) * FF_CHUNK)
    val_cols = lambda c: slice(D_FF + c * FF_CHUNK, D_FF + (c + 1) * FF_CHUNK)

    @pl.when(pl.program_id(1) == 0)
    def _():
        carryg_ref[...] = jnp.zeros_like(carryg_ref)
        carryv_ref[...] = jnp.zeros_like(carryv_ref)

    def attn_out(rs):
        return lax.dot_general(attn_ref[0, :, rs], watt_ref[...], _TN_DIMS, preferred_element_type=_F32)

    def gate_mix(rs, y_att):
        t = gatt_ref[0, rs, :].astype(_F32) * y_att + cnvg_ref[0, rs, :].astype(_F32)
        x1b_ref[rs, :] = t.astype(_BF16)

    def mix_proj(rs):
        return jnp.dot(x1b_ref[rs, :], wo_ref[...], preferred_element_type=_F32)

    def norm1(rs, mix):
        x1 = _layer_norm(DEEPNORM_ALPHA * x_ref[0, rs, :] + mix, g1_ref[...], b1_ref[...])
        x1_ref[rs, :] = x1
        x1b_ref[rs, :] = x1.astype(_BF16)

    units = [(rs, c) for rs in halves for c in range(n_chunks)]

    def up_proj(n):
        rs, c = units[n]
        u_ref[n % 2, 0] = jnp.dot(x1b_ref[rs, :], wup_ref[:, gate_cols(c)], preferred_element_type=_F32)
        u_ref[n % 2, 1] = jnp.dot(x1b_ref[rs, :], wup_ref[:, val_cols(c)], preferred_element_type=_F32)

    def conv(u, cols, carry_ref, c):
        out = _causal_conv3(u, carry_ref[c], cw_ref[:, cols])
        carry_ref[c] = u[half - SUBLANES:half]
        return out

    def gated_conv(n):
        rs, c = units[n]
        cg = conv(u_ref[n % 2, 0], gate_cols(c), carryg_ref, c)
        cv = conv(u_ref[n % 2, 1], val_cols(c), carryv_ref, c)
        hmid_ref[c, rs, :] = _silu_times(cg, cv).astype(_BF16)

    def down_proj(rs):
        f = jnp.dot(hmid_ref[0, rs, :], wdown_ref[gate_cols(0), :], preferred_element_type=_F32)
        for c in range(1, n_chunks):
            f = f + jnp.dot(hmid_ref[c, rs, :], wdown_ref[gate_cols(c), :], preferred_element_type=_F32)
        return f

    def norm2(rs, f):
        o_ref[0, rs, :] = _layer_norm(DEEPNORM_ALPHA * x1_ref[rs, :] + f, g2_ref[...], b2_ref[...])

    ha, hb = halves
    ya = attn_out(ha)
    yb = attn_out(hb)
    gate_mix(ha, ya)
    mix_a = mix_proj(ha)
    gate_mix(hb, yb)
    mix_b = mix_proj(hb)
    norm1(ha, mix_a)
    up_proj(0)
    norm1(hb, mix_b)
    for n in range(len(units)):
        if n + 1 < len(units):
            up_proj(n + 1)
        gated_conv(n)
    fa = down_proj(ha)
    fb = down_proj(hb)
    norm2(ha, fa)
    norm2(hb, fb)


def _post(x, attn, gatt, cnvg, watt, wo, g1, b1, wup, cw, wdown, g2, b2):
    bsz, s, d = x.shape
    tile = TOKEN_TILE
    n_chunks = D_FF // FF_CHUNK
    tok = lambda w: pl.BlockSpec((1, tile, w), lambda b, t: (b, t, 0))
    weights = (watt, wo, g1, b1, wup, cw, wdown, g2, b2)
    return pl.pallas_call(
        _post_kernel,
        grid=(bsz, s // tile),
        in_specs=[tok(d), pl.BlockSpec((1, ATTN_WIDTH, tile), lambda b, t: (b, 0, t)), tok(d), tok(d)]
        + [_resident(w.shape) for w in weights],
        out_specs=tok(d),
        out_shape=jax.ShapeDtypeStruct((bsz, s, d), x.dtype),
        scratch_shapes=[
            pltpu.VMEM((tile, d), _F32),
            pltpu.VMEM((tile, d), _BF16),
            pltpu.VMEM((2, 2, tile // 2, FF_CHUNK), _F32),
            pltpu.VMEM((n_chunks, SUBLANES, FF_CHUNK), _F32),
            pltpu.VMEM((n_chunks, SUBLANES, FF_CHUNK), _F32),
            pltpu.VMEM((n_chunks, tile, FF_CHUNK), _BF16),
        ],
        compiler_params=pltpu.CompilerParams(
            dimension_semantics=("arbitrary", "arbitrary"), vmem_limit_bytes=VMEM_LIMIT_BYTES),
        name="post",
    )(x, attn, gatt, cnvg, *weights)


def kernel(x, positions, w_in, b_gate, w_attn_out, conv_w_mix, w_conv_out, w_o,
           ln1_g, ln1_b, w_up, conv_w_ffn, w_down, ln2_g, ln2_b):
    bsz, s, d = x.shape
    assert s % TOKEN_TILE == 0 and TOKEN_TILE % MOBA_BLOCK == 0 and D_FF % FF_CHUNK == 0
    a = ATTN_WIDTH
    inv_freq = (ROPE_THETA ** (-jnp.arange(0, ROT_DIM, 2, dtype=_F32) / ROT_DIM)).reshape(ROT_HALF, 1)
    pos3 = positions.reshape(bsz, 1, s)
    for l in range(DEPTH):
        wqkvt = w_in[l, :, :3 * a].T.astype(_BF16)
        wrest = w_in[l, :, 3 * a:].astype(_BF16)
        qt, k, vt, gatt, cnvg = _inproj(
            x, pos3, inv_freq, wqkvt, wrest, b_gate[l].reshape(1, -1), conv_w_mix[l],
            w_conv_out[l].astype(_BF16))
        attn = _moba(qt, k, vt)
        row = lambda v: v.reshape(1, -1)
        x = _post(
            x, attn, gatt, cnvg, w_attn_out[l].astype(_BF16), w_o[l].astype(_BF16),
            row(ln1_g[l]), row(ln1_b[l]),
            w_up[l].astype(_BF16), conv_w_ffn[l], w_down[l].astype(_BF16),
            row(ln2_g[l]), row(ln2_b[l]))
    return x
```

```python
import math

import jax
import jax.numpy as jnp
from jax import lax
from jax.experimental import pallas as pl
from jax.experimental.pallas import tpu as pltpu

N_HEADS = 8
HEAD_DIM = 64
ATTN_WIDTH = N_HEADS * HEAD_DIM
MOBA_BLOCK = 256
MOBA_TOPK = 3
ROPE_THETA = 500000.0
ROT_DIM = HEAD_DIM // 4
ROT_HALF = ROT_DIM // 2
CONV_WIDTH = 512
CONV_K = 3
D_FF = 2816
DEPTH = 1
DEEPNORM_ALPHA = (2.0 * DEPTH) ** 0.25
LN_EPS = 1e-5

Q_SCALE = (1.0 / math.sqrt(HEAD_DIM)) * math.log2(math.e)

HEADS_PER_GROUP = 2
GROUP_WIDTH = HEADS_PER_GROUP * HEAD_DIM
N_GROUPS = N_HEADS // HEADS_PER_GROUP
MOBA_GROUPS_PER_STEP = 4
SCORE_LOOKAHEAD = 4
BF16_SUBLANES = 16
ACC_ROWS = HEAD_DIM + BF16_SUBLANES
SUBLANES = 8
TOKEN_TILE = 512
FF_CHUNK = 256
VMEM_LIMIT_BYTES = 56 * 1024 * 1024

_NT_DIMS = (((1,), (1,)), ((), ()))
_F32 = jnp.float32
_BF16 = jnp.bfloat16


def _resident(shape):
    nd = len(shape)
    return pl.BlockSpec(shape, lambda *_: (0,) * nd, pipeline_mode=pl.Buffered(1))


def _causal_conv3(cur, carry, w):
    rows, width = cur.shape
    groups = rows // SUBLANES
    is_last = lax.broadcasted_iota(jnp.int32, (groups, SUBLANES, width), 1) == SUBLANES - 1

    def shift1(a, a_before):
        a3 = a.reshape(groups, SUBLANES, width)
        prev3 = jnp.concatenate([a_before[None], a3[:-1]], axis=0)
        return pltpu.roll(jnp.where(is_last, prev3, a3), 1, axis=1).reshape(rows, width)

    inner = w[1:2] * cur + shift1(w[0:1] * cur, w[0:1] * carry)
    inner_before = w[1:2] * carry + w[0:1] * pltpu.roll(carry, 1, axis=0)
    return w[2:3] * cur + shift1(inner, inner_before)


def _tree_reduce(fn, xs):
    while len(xs) > 1:
        xs = [fn(xs[k], xs[k + 1]) for k in range(0, len(xs) - 1, 2)] + (xs[-1:] if len(xs) % 2 else [])
    return xs[0]


def _sigmoid(x):
    return 0.5 * jnp.tanh(0.5 * x) + 0.5


def _silu_times(g, v):
    h = 0.5 * g
    return h * (jnp.tanh(h) + 1.0) * v


def _layer_norm(z, g, b):
    mu = jnp.mean(z, axis=-1, keepdims=True)
    zc = z - mu
    var = jnp.mean(zc * zc, axis=-1, keepdims=True)
    return zc * lax.rsqrt(var + LN_EPS) * g + b


def _inproj_kernel(x_ref, pos_ref, invf_ref, wqkvt_ref, wrest_ref, bg_ref, cw_ref, wcnv_ref,
                   qt_ref, k_ref, vt_ref, gatt_ref, cnvg_ref, carry_ref):
    tile = x_ref.shape[1]
    d_model = x_ref.shape[2]
    c = CONV_WIDTH
    xb = x_ref[0].astype(_BF16)

    gates_pre = jnp.dot(xb, wrest_ref[:, 3 * c:], preferred_element_type=_F32)
    qkvt = lax.dot_general(wqkvt_ref[...], xb, _NT_DIMS, preferred_element_type=_F32)
    gates = _sigmoid(gates_pre + bg_ref[...])
    gatt_ref[0] = gates[:, :d_model].astype(_BF16)
    g_cnv = gates[:, d_model:]

    rest = jnp.dot(xb, wrest_ref[:, 0:3 * c], preferred_element_type=_F32)
    ang = invf_ref[...] * pos_ref[0].astype(_F32)
    cos, sin = jnp.cos(ang), jnp.sin(ang)

    def rope_t(t):
        parts = []
        for h in range(N_HEADS):
            r = h * HEAD_DIM
            x1 = t[r:r + ROT_HALF]
            x2 = t[r + ROT_HALF:r + ROT_DIM]
            parts += [x1 * cos - x2 * sin, x2 * cos + x1 * sin, t[r + ROT_DIM:r + HEAD_DIM]]
        return jnp.concatenate(parts, axis=0)

    a = ATTN_WIDTH
    qt_ref[0] = (rope_t(qkvt[0:a]) * Q_SCALE).astype(_BF16)
    k_ref[0] = rope_t(qkvt[a:2 * a]).T.astype(_BF16)
    vt = qkvt[2 * a:3 * a].astype(_BF16)
    for j in range(tile // MOBA_BLOCK):
        vt_ref[0, j] = vt[:, j * MOBA_BLOCK:(j + 1) * MOBA_BLOCK]

    gate_b, gate_c, h_in = rest[:, 0:c], rest[:, c:2 * c], rest[:, 2 * c:3 * c]
    ch = gate_c * h_in

    @pl.when(pl.program_id(1) == 0)
    def _():
        carry_ref[...] = jnp.zeros_like(carry_ref)

    conv = _causal_conv3(ch, carry_ref[...], cw_ref[...])
    carry_ref[...] = ch[tile - SUBLANES:tile]
    y_cnv = jnp.dot((gate_b * conv).astype(_BF16), wcnv_ref[...], preferred_element_type=_F32)
    cnvg_ref[0] = (g_cnv * y_cnv).astype(_BF16)


def _inproj(x, pos3, inv_freq, wqkvt, wrest, bg, cw, wcnv):
    bsz, s, d = x.shape
    tile = TOKEN_TILE
    nb = s // MOBA_BLOCK
    a = ATTN_WIDTH
    tok = lambda w: pl.BlockSpec((1, tile, w), lambda b, t: (b, t, 0))
    return pl.pallas_call(
        _inproj_kernel,
        grid=(bsz, s // tile),
        in_specs=[
            tok(d),
            pl.BlockSpec((1, 1, tile), lambda b, t: (b, 0, t)),
            _resident(inv_freq.shape),
            _resident(wqkvt.shape),
            _resident(wrest.shape),
            _resident(bg.shape),
            _resident(cw.shape),
            _resident(wcnv.shape),
        ],
        out_specs=[
            pl.BlockSpec((1, a, tile), lambda b, t: (b, 0, t)),
            tok(a),
            pl.BlockSpec((1, tile // MOBA_BLOCK, a, MOBA_BLOCK), lambda b, t: (b, t, 0, 0)),
            tok(d),
            tok(d),
        ],
        out_shape=[
            jax.ShapeDtypeStruct((bsz, a, s), _BF16),
            jax.ShapeDtypeStruct((bsz, s, a), _BF16),
            jax.ShapeDtypeStruct((bsz, nb, a, MOBA_BLOCK), _BF16),
            jax.ShapeDtypeStruct((bsz, s, d), _BF16),
            jax.ShapeDtypeStruct((bsz, s, d), _BF16),
        ],
        scratch_shapes=[pltpu.VMEM((SUBLANES, CONV_WIDTH), _F32)],
        compiler_params=pltpu.CompilerParams(
            dimension_semantics=("arbitrary", "arbitrary"), vmem_limit_bytes=VMEM_LIMIT_BYTES),
        name="inproj",
    )(x, pos3, inv_freq, wqkvt, wrest, bg, cw, wcnv)


def _moba_kernel(qt_ref, k_ref, vt_ref, o_ref, kmean_ref, mrow_ref, qm_ref, s_ref, p_ref, acc_ref,
                 m_ref, cmax_ref, alpha_ref):
    i = pl.program_id(2)
    nb = vt_ref.shape[1]
    blk = MOBA_BLOCK
    neg_inf = -jnp.inf

    n_heads = qt_ref.shape[1] // HEAD_DIM
    width = qt_ref.shape[1]
    gw = GROUP_WIDTH

    @pl.when(i == 0)
    def _():
        chan = lax.broadcasted_iota(jnp.int32, (n_heads, width), 1)
        head0 = lax.broadcasted_iota(jnp.int32, (n_heads, width), 0) * HEAD_DIM
        in_head = (chan >= head0) & (chan < head0 + HEAD_DIM)

        def mean_body(j, _):
            kb = k_ref[0, pl.ds(pl.multiple_of(j * blk, blk), blk), :].astype(_F32)
            km = jnp.sum(kb, axis=0, keepdims=True) / blk
            kmean_ref[pl.ds(pl.multiple_of(j * n_heads, n_heads), n_heads), :] = jnp.where(in_head, km, 0.0)
            return 0
        lax.fori_loop(0, nb, mean_body, 0)

    grow = lax.broadcasted_iota(jnp.int32, (gw, blk), 0)
    for h in range(n_heads):
        g, hh = divmod(h, HEADS_PER_GROUP)
        qg = qt_ref[0, g * gw:(g + 1) * gw, :]
        qm_ref[h] = jnp.where((grow >= hh * HEAD_DIM) & (grow < (hh + 1) * HEAD_DIM), qg, jnp.zeros_like(qg))
    gsl = lambda h: slice((h // HEADS_PER_GROUP) * gw, (h // HEADS_PER_GROUP + 1) * gw)

    sc_all = jnp.dot(kmean_ref[...].astype(_BF16), qt_ref[0], preferred_element_type=_F32)
    scs = [jnp.where(j < i, sc_all[j * n_heads:(j + 1) * n_heads], neg_inf) for j in range(nb)]
    offs = [jnp.full((n_heads, blk), jnp.inf, _F32)] * nb
    for _ in range(MOBA_TOPK):
        best = _tree_reduce(jnp.maximum, scs)
        idx = _tree_reduce(jnp.minimum, [jnp.where(scs[j] == best, j, nb) for j in range(nb)])
        picks = [idx == j for j in range(nb)]
        offs = [jnp.where(picks[j], 0.0, offs[j]) for j in range(nb)]
        scs = [jnp.where(picks[j], neg_inf, scs[j]) for j in range(nb)]
    for j in range(nb):
        mrow_ref[j] = jnp.where(j < i, offs[j], jnp.inf)

    last = n_heads - 1
    for h in range(n_heads):
        acc_ref[h] = jnp.zeros((ACC_ROWS, blk), _F32)
    p_ref[last] = jnp.zeros((blk, blk), _BF16)
    alpha_ref[last] = jnp.ones((1, blk), _F32)

    kpos = lax.broadcasted_iota(jnp.int32, (blk, blk), 0)
    qpos = lax.broadcasted_iota(jnp.int32, (blk, blk), 1)
    ones_rows = jnp.ones((ACC_ROWS - HEAD_DIM, blk), _BF16)

    def scores(j, h, own):
        kj = k_ref[0, pl.ds(pl.multiple_of(j * blk, blk), blk), gsl(h)]
        s = jnp.dot(kj, qm_ref[h], preferred_element_type=_F32)
        if own:
            s = jnp.where(kpos <= qpos, s, neg_inf)
        s_ref[h] = s
        cmax_ref[h] = jnp.max(s, axis=0, keepdims=True)

    def expo(j, h, own):
        if own:
            m_new = cmax_ref[h]
            shift = m_new
            alpha_ref[h] = jnp.zeros((1, blk), _F32)
        else:
            m_old = m_ref[h]
            off = mrow_ref[j, pl.ds(h, 1), :]
            m_new = jnp.maximum(m_old, cmax_ref[h] - off)
            shift = m_new + off
            alpha_ref[h] = jnp.exp2(m_old - m_new)
        m_ref[h] = m_new
        p_ref[h] = jnp.exp2(s_ref[h] - shift).astype(_BF16)

    def pv_unit(j, h):
        vt1 = jnp.concatenate([vt_ref[0, j, h * HEAD_DIM:(h + 1) * HEAD_DIM, :], ones_rows], axis=0)
        pv = jnp.dot(vt1, p_ref[h], preferred_element_type=_F32)
        acc_ref[h] = alpha_ref[h] * acc_ref[h] + pv

    def process_block(j, own, j_next, j_prev):
        for h in range(n_heads):
            ahead = h + SCORE_LOOKAHEAD
            if ahead < n_heads:
                scores(j, ahead, own)
            else:
                scores(j_next, ahead - n_heads, False)
            expo(j, h, own)
            if h == 0:
                pv_unit(j_prev, last)
            else:
                pv_unit(j, h - 1)

    for h in range(SCORE_LOOKAHEAD):
        scores(i, h, True)
    process_block(i, True, 0, i)

    def past_block(j):
        process_block(j, False, jnp.minimum(j + 1, nb - 1), jnp.where(j == 0, i, j - 1))

    def blocks_body(per_trip):
        def body(t, carry):
            for r in range(per_trip):
                past_block(per_trip * t + r)
            return carry
        return body

    n_octs = lax.shift_right_logical(i, 3)
    n_quads = lax.shift_right_logical(i, 2)
    n_pairs = lax.shift_right_logical(i, 1)
    lax.fori_loop(0, n_octs, blocks_body(8), 0)
    lax.fori_loop(2 * n_octs, n_quads, blocks_body(4), 0)
    lax.fori_loop(2 * n_quads, n_pairs, blocks_body(2), 0)
    lax.fori_loop(2 * n_pairs, i, blocks_body(1), 0)
    pv_unit(jnp.where(i == 0, i, i - 1), last)

    ot = jnp.concatenate(
        [acc_ref[h, 0:HEAD_DIM, :] / acc_ref[h, HEAD_DIM:HEAD_DIM + 1, :] for h in range(n_heads)], axis=0)
    o_ref[0] = ot.T.astype(_BF16)


def _moba(qt, k, vt):
    bsz, a, s = qt.shape
    nb = s // MOBA_BLOCK
    gw = GROUP_WIDTH * MOBA_GROUPS_PER_STEP
    n_heads = HEADS_PER_GROUP * MOBA_GROUPS_PER_STEP
    assert SCORE_LOOKAHEAD < n_heads and n_heads == SUBLANES
    return pl.pallas_call(
        _moba_kernel,
        grid=(bsz, N_GROUPS // MOBA_GROUPS_PER_STEP, nb),
        in_specs=[
            pl.BlockSpec((1, gw, MOBA_BLOCK), lambda b, g, i: (b, g, i)),
            pl.BlockSpec((1, s, gw), lambda b, g, i: (b, 0, g)),
            pl.BlockSpec((1, nb, gw, MOBA_BLOCK), lambda b, g, i: (b, 0, g, 0)),
        ],
        out_specs=pl.BlockSpec((1, MOBA_BLOCK, gw), lambda b, g, i: (b, i, g)),
        out_shape=jax.ShapeDtypeStruct((bsz, s, a), _BF16),
        scratch_shapes=[
            pltpu.VMEM((nb * n_heads, gw), _F32),
            pltpu.VMEM((nb, n_heads, MOBA_BLOCK), _F32),
            pltpu.VMEM((n_heads, GROUP_WIDTH, MOBA_BLOCK), _BF16),
            pltpu.VMEM((n_heads, MOBA_BLOCK, MOBA_BLOCK), _F32),
            pltpu.VMEM((n_heads, MOBA_BLOCK, MOBA_BLOCK), _BF16),
            pltpu.VMEM((n_heads, ACC_ROWS, MOBA_BLOCK), _F32),
            pltpu.VMEM((n_heads, 1, MOBA_BLOCK), _F32),
            pltpu.VMEM((n_heads, 1, MOBA_BLOCK), _F32),
            pltpu.VMEM((n_heads, 1, MOBA_BLOCK), _F32),
        ],
        compiler_params=pltpu.CompilerParams(
            dimension_semantics=("arbitrary", "arbitrary", "arbitrary"), vmem_limit_bytes=VMEM_LIMIT_BYTES),
        name="moba",
    )(qt, k, vt)


def _post_kernel(x_ref, attn_ref, gatt_ref, cnvg_ref, watt_ref, wo_ref, g1_ref, b1_ref,
                 wup_ref, cw_ref, wdown_ref, g2_ref, b2_ref,
                 o_ref, x1_ref, x1b_ref, u_ref, carryg_ref, carryv_ref, hmid_ref):
    tile = x_ref.shape[1]
    half = tile // 2
    halves = (slice(0, half), slice(half, tile))
    n_chunks = D_FF // FF_CHUNK
    gate_cols = lambda c: slice(c * FF_CHUNK, (c + 1) * FF_CHUNK)
    val_cols = lambda c: slice(D_FF + c * FF_CHUNK, D_FF + (c + 1) * FF_CHUNK)

    @pl.when(pl.program_id(1) == 0)
    def _():
        carryg_ref[...] = jnp.zeros_like(carryg_ref)
        carryv_ref[...] = jnp.zeros_like(carryv_ref)

    def attn_out(rs):
        return jnp.dot(attn_ref[0, rs, :], watt_ref[...], preferred_element_type=_F32)

    def gate_mix(rs, y_att):
        t = gatt_ref[0, rs, :].astype(_F32) * y_att + cnvg_ref[0, rs, :].astype(_F32)
        x1b_ref[rs, :] = t.astype(_BF16)

    def mix_proj(rs):
        return jnp.dot(x1b_ref[rs, :], wo_ref[...], preferred_element_type=_F32)

    def norm1(rs, mix):
        x1 = _layer_norm(DEEPNORM_ALPHA * x_ref[0, rs, :] + mix, g1_ref[...], b1_ref[...])
        x1_ref[rs, :] = x1
        x1b_ref[rs, :] = x1.astype(_BF16)

    units = [(rs, c) for rs in halves for c in range(n_chunks)]

    def up_proj(n):
        rs, c = units[n]
        u_ref[n % 2, 0] = jnp.dot(x1b_ref[rs, :], wup_ref[:, gate_cols(c)], preferred_element_type=_F32)
        u_ref[n % 2, 1] = jnp.dot(x1b_ref[rs, :], wup_ref[:, val_cols(c)], preferred_element_type=_F32)

    def conv(u, cols, carry_ref, c):
        out = _causal_conv3(u, carry_ref[c], cw_ref[:, cols])
        carry_ref[c] = u[half - SUBLANES:half]
        return out

    def gated_conv(n):
        rs, c = units[n]
        cg = conv(u_ref[n % 2, 0], gate_cols(c), carryg_ref, c)
        cv = conv(u_ref[n % 2, 1], val_cols(c), carryv_ref, c)
        hmid_ref[c, rs, :] = _silu_times(cg, cv).astype(_BF16)

    def down_proj(rs):
        f = jnp.dot(hmid_ref[0, rs, :], wdown_ref[gate_cols(0), :], preferred_element_type=_F32)
        for c in range(1, n_chunks):
            f = f + jnp.dot(hmid_ref[c, rs, :], wdown_ref[gate_cols(c), :], preferred_element_type=_F32)
        return f

    def norm2(rs, f):
        o_ref[0, rs, :] = _layer_norm(DEEPNORM_ALPHA * x1_ref[rs, :] + f, g2_ref[...], b2_ref[...])

    ha, hb = halves
    ya = attn_out(ha)
    yb = attn_out(hb)
    gate_mix(ha, ya)
    mix_a = mix_proj(ha)
    gate_mix(hb, yb)
    mix_b = mix_proj(hb)
    norm1(ha, mix_a)
    up_proj(0)
    norm1(hb, mix_b)
    for n in range(len(units)):
        if n + 1 < len(units):
            up_proj(n + 1)
        gated_conv(n)
    fa = down_proj(ha)
    fb = down_proj(hb)
    norm2(ha, fa)
    norm2(hb, fb)


def _post(x, attn, gatt, cnvg, watt, wo, g1, b1, wup, cw, wdown, g2, b2):
    bsz, s, d = x.shape
    tile = TOKEN_TILE
    n_chunks = D_FF // FF_CHUNK
    tok = lambda w: pl.BlockSpec((1, tile, w), lambda b, t: (b, t, 0))
    weights = (watt, wo, g1, b1, wup, cw, wdown, g2, b2)
    return pl.pallas_call(
        _post_kernel,
        grid=(bsz, s // tile),
        in_specs=[tok(d), tok(ATTN_WIDTH), tok(d), tok(d)] + [_resident(w.shape) for w in weights],
        out_specs=tok(d),
        out_shape=jax.ShapeDtypeStruct((bsz, s, d), x.dtype),
        scratch_shapes=[
            pltpu.VMEM((tile, d), _F32),
            pltpu.VMEM((tile, d), _BF16),
            pltpu.VMEM((2, 2, tile // 2, FF_CHUNK), _F32),
            pltpu.VMEM((n_chunks, SUBLANES, FF_CHUNK), _F32),
            pltpu.VMEM((n_chunks, SUBLANES, FF_CHUNK), _F32),
            pltpu.VMEM((n_chunks, tile, FF_CHUNK), _BF16),
        ],
        compiler_params=pltpu.CompilerParams(
            dimension_semantics=("arbitrary", "arbitrary"), vmem_limit_bytes=VMEM_LIMIT_BYTES),
        name="post",
    )(x, attn, gatt, cnvg, *weights)


def kernel(x, positions, w_in, b_gate, w_attn_out, conv_w_mix, w_conv_out, w_o,
           ln1_g, ln1_b, w_up, conv_w_ffn, w_down, ln2_g, ln2_b):
    bsz, s, d = x.shape
    assert s % TOKEN_TILE == 0 and TOKEN_TILE % MOBA_BLOCK == 0 and D_FF % FF_CHUNK == 0
    a = ATTN_WIDTH
    inv_freq = (ROPE_THETA ** (-jnp.arange(0, ROT_DIM, 2, dtype=_F32) / ROT_DIM)).reshape(ROT_HALF, 1)
    pos3 = positions.reshape(bsz, 1, s)
    for l in range(DEPTH):
        wqkvt = w_in[l, :, :3 * a].T.astype(_BF16)
        wrest = w_in[l, :, 3 * a:].astype(_BF16)
        qt, k, vt, gatt, cnvg = _inproj(
            x, pos3, inv_freq, wqkvt, wrest, b_gate[l].reshape(1, -1), conv_w_mix[l],
            w_conv_out[l].astype(_BF16))
        attn = _moba(qt, k, vt)
        row = lambda v: v.reshape(1, -1)
        x = _post(
            x, attn, gatt, cnvg, w_attn_out[l].astype(_BF16), w_o[l].astype(_BF16),
            row(ln1_g[l]), row(ln1_b[l]),
            w_up[l].astype(_BF16), conv_w_ffn[l], w_down[l].astype(_BF16),
            row(ln2_g[l]), row(ln2_b[l]))
    return x
```

```python
import functools
import math

import jax
import jax.numpy as jnp
from jax import lax
from jax.experimental import pallas as pl
from jax.experimental.pallas import tpu as pltpu

N_HEADS = 8
HEAD_DIM = 64
ATTN_WIDTH = N_HEADS * HEAD_DIM
MOBA_BLOCK = 256
MOBA_TOPK = 3
ROPE_THETA = 500000.0
ROT_DIM = HEAD_DIM // 4
ROT_HALF = ROT_DIM // 2
CONV_WIDTH = 512
CONV_K = 3
D_FF = 2816
DEPTH = 1
DEEPNORM_ALPHA = (2.0 * DEPTH) ** 0.25
LN_EPS = 1e-5

Q_SCALE = (1.0 / math.sqrt(HEAD_DIM)) * math.log2(math.e)

HEADS_PER_GROUP = 2
GROUP_WIDTH = HEADS_PER_GROUP * HEAD_DIM
N_GROUPS = N_HEADS // HEADS_PER_GROUP
MOBA_GROUPS_PER_STEP = 4
MOBA_Q_BLOCKS_PER_STEP = 2
SCORE_LOOKAHEAD = 4
SOFTMAX_M_INIT = -1e30
BF16_SUBLANES = 16
ACC_ROWS = HEAD_DIM + BF16_SUBLANES
SUBLANES = 8
TOKEN_TILE = 512
FF_CHUNK = 256
VMEM_LIMIT_BYTES = 56 * 1024 * 1024

_NT_DIMS = (((1,), (1,)), ((), ()))
_F32 = jnp.float32
_BF16 = jnp.bfloat16


def _resident(shape):
    nd = len(shape)
    return pl.BlockSpec(shape, lambda *_: (0,) * nd, pipeline_mode=pl.Buffered(1))


def _causal_conv3(cur, carry, w):
    rows, width = cur.shape
    groups = rows // SUBLANES
    is_last = lax.broadcasted_iota(jnp.int32, (groups, SUBLANES, width), 1) == SUBLANES - 1

    def shift1(a, a_before):
        a3 = a.reshape(groups, SUBLANES, width)
        prev3 = jnp.concatenate([a_before[None], a3[:-1]], axis=0)
        return pltpu.roll(jnp.where(is_last, prev3, a3), 1, axis=1).reshape(rows, width)

    inner = w[1:2] * cur + shift1(w[0:1] * cur, w[0:1] * carry)
    inner_before = w[1:2] * carry + w[0:1] * pltpu.roll(carry, 1, axis=0)
    return w[2:3] * cur + shift1(inner, inner_before)


def _tree_reduce(fn, xs):
    while len(xs) > 1:
        xs = [fn(xs[k], xs[k + 1]) for k in range(0, len(xs) - 1, 2)] + (xs[-1:] if len(xs) % 2 else [])
    return xs[0]


def _sigmoid(x):
    return 0.5 * jnp.tanh(0.5 * x) + 0.5


def _silu_times(g, v):
    h = 0.5 * g
    return h * (jnp.tanh(h) + 1.0) * v


def _layer_norm(z, g, b):
    mu = jnp.mean(z, axis=-1, keepdims=True)
    zc = z - mu
    var = jnp.mean(zc * zc, axis=-1, keepdims=True)
    return zc * lax.rsqrt(var + LN_EPS) * g + b


def _inproj_kernel(x_ref, pos_ref, invf_ref, wqkvt_ref, wrest_ref, bg_ref, cw_ref, wcnv_ref,
                   qt_ref, k_ref, vt_ref, gatt_ref, cnvg_ref, carry_ref):
    tile = x_ref.shape[1]
    d_model = x_ref.shape[2]
    c = CONV_WIDTH
    xb = x_ref[0].astype(_BF16)

    gates_pre = jnp.dot(xb, wrest_ref[:, 3 * c:], preferred_element_type=_F32)
    qkvt = lax.dot_general(wqkvt_ref[...], xb, _NT_DIMS, preferred_element_type=_F32)
    gates = _sigmoid(gates_pre + bg_ref[...])
    gatt_ref[0] = gates[:, :d_model].astype(_BF16)
    g_cnv = gates[:, d_model:]

    rest = jnp.dot(xb, wrest_ref[:, 0:3 * c], preferred_element_type=_F32)
    ang = invf_ref[...] * pos_ref[0].astype(_F32)
    cos, sin = jnp.cos(ang), jnp.sin(ang)

    def rope_t(t):
        parts = []
        for h in range(N_HEADS):
            r = h * HEAD_DIM
            x1 = t[r:r + ROT_HALF]
            x2 = t[r + ROT_HALF:r + ROT_DIM]
            parts += [x1 * cos - x2 * sin, x2 * cos + x1 * sin, t[r + ROT_DIM:r + HEAD_DIM]]
        return jnp.concatenate(parts, axis=0)

    a = ATTN_WIDTH
    qt_ref[0] = (rope_t(qkvt[0:a]) * Q_SCALE).astype(_BF16)
    k_ref[0] = rope_t(qkvt[a:2 * a]).T.astype(_BF16)
    vt = qkvt[2 * a:3 * a].astype(_BF16)
    for j in range(tile // MOBA_BLOCK):
        vt_ref[0, j] = vt[:, j * MOBA_BLOCK:(j + 1) * MOBA_BLOCK]

    gate_b, gate_c, h_in = rest[:, 0:c], rest[:, c:2 * c], rest[:, 2 * c:3 * c]
    ch = gate_c * h_in

    @pl.when(pl.program_id(1) == 0)
    def _():
        carry_ref[...] = jnp.zeros_like(carry_ref)

    conv = _causal_conv3(ch, carry_ref[...], cw_ref[...])
    carry_ref[...] = ch[tile - SUBLANES:tile]
    y_cnv = jnp.dot((gate_b * conv).astype(_BF16), wcnv_ref[...], preferred_element_type=_F32)
    cnvg_ref[0] = (g_cnv * y_cnv).astype(_BF16)


def _inproj(x, pos3, inv_freq, wqkvt, wrest, bg, cw, wcnv):
    bsz, s, d = x.shape
    tile = TOKEN_TILE
    nb = s // MOBA_BLOCK
    a = ATTN_WIDTH
    tok = lambda w: pl.BlockSpec((1, tile, w), lambda b, t: (b, t, 0))
    return pl.pallas_call(
        _inproj_kernel,
        grid=(bsz, s // tile),
        in_specs=[
            tok(d),
            pl.BlockSpec((1, 1, tile), lambda b, t: (b, 0, t)),
            _resident(inv_freq.shape),
            _resident(wqkvt.shape),
            _resident(wrest.shape),
            _resident(bg.shape),
            _resident(cw.shape),
            _resident(wcnv.shape),
        ],
        out_specs=[
            pl.BlockSpec((1, a, tile), lambda b, t: (b, 0, t)),
            tok(a),
            pl.BlockSpec((1, tile // MOBA_BLOCK, a, MOBA_BLOCK), lambda b, t: (b, t, 0, 0)),
            tok(d),
            tok(d),
        ],
        out_shape=[
            jax.ShapeDtypeStruct((bsz, a, s), _BF16),
            jax.ShapeDtypeStruct((bsz, s, a), _BF16),
            jax.ShapeDtypeStruct((bsz, nb, a, MOBA_BLOCK), _BF16),
            jax.ShapeDtypeStruct((bsz, s, d), _BF16),
            jax.ShapeDtypeStruct((bsz, s, d), _BF16),
        ],
        scratch_shapes=[pltpu.VMEM((SUBLANES, CONV_WIDTH), _F32)],
        compiler_params=pltpu.CompilerParams(
            dimension_semantics=("arbitrary", "arbitrary"), vmem_limit_bytes=VMEM_LIMIT_BYTES),
        name="inproj",
    )(x, pos3, inv_freq, wqkvt, wrest, bg, cw, wcnv)


def _moba_kernel(qt_ref, *refs):
    step = pl.program_id(2)
    for r in range(MOBA_Q_BLOCKS_PER_STEP):
        first_of_batch = (step == 0) if r == 0 else None
        _moba_query_block(step * MOBA_Q_BLOCKS_PER_STEP + r, slice(r * MOBA_BLOCK, (r + 1) * MOBA_BLOCK),
                          first_of_batch, qt_ref, *refs)


def _moba_query_block(i, qs, first_of_batch, qt_ref, k_ref, vt_ref, o_ref, kmean_ref, mrow_ref, qm_ref, s_ref,
                      p_ref, acc_ref, m_ref, cmax_ref, alpha_ref):
    nb = vt_ref.shape[1]
    blk = MOBA_BLOCK
    neg_inf = -jnp.inf

    n_heads = qt_ref.shape[1] // HEAD_DIM
    width = qt_ref.shape[1]
    gw = GROUP_WIDTH

    def block_means():
        chan = lax.broadcasted_iota(jnp.int32, (n_heads, width), 1)
        head0 = lax.broadcasted_iota(jnp.int32, (n_heads, width), 0) * HEAD_DIM
        in_head = (chan >= head0) & (chan < head0 + HEAD_DIM)

        def mean_body(j, _):
            kb = k_ref[0, pl.ds(pl.multiple_of(j * blk, blk), blk), :].astype(_F32)
            km = jnp.sum(kb, axis=0, keepdims=True) / blk
            kmean_ref[pl.ds(pl.multiple_of(j * n_heads, n_heads), n_heads), :] = jnp.where(in_head, km, 0.0)
            return 0
        lax.fori_loop(0, nb, mean_body, 0)

    if first_of_batch is not None:
        pl.when(first_of_batch)(block_means)

    grow = lax.broadcasted_iota(jnp.int32, (gw, blk), 0)
    for h in range(n_heads):
        g, hh = divmod(h, HEADS_PER_GROUP)
        qg = qt_ref[0, g * gw:(g + 1) * gw, qs]
        qm_ref[h] = jnp.where((grow >= hh * HEAD_DIM) & (grow < (hh + 1) * HEAD_DIM), qg, jnp.zeros_like(qg))
    gsl = lambda h: slice((h // HEADS_PER_GROUP) * gw, (h // HEADS_PER_GROUP + 1) * gw)

    sc_all = jnp.dot(kmean_ref[...].astype(_BF16), qt_ref[0, :, qs], preferred_element_type=_F32)
    scs = [jnp.where(j < i, sc_all[j * n_heads:(j + 1) * n_heads], neg_inf) for j in range(nb)]
    offs = [jnp.full((n_heads, blk), jnp.inf, _F32)] * nb
    for _ in range(MOBA_TOPK):
        best = _tree_reduce(jnp.maximum, scs)
        idx = _tree_reduce(jnp.minimum, [jnp.where(scs[j] == best, j, nb) for j in range(nb)])
        picks = [idx == j for j in range(nb)]
        offs = [jnp.where(picks[j], 0.0, offs[j]) for j in range(nb)]
        scs = [jnp.where(picks[j], neg_inf, scs[j]) for j in range(nb)]
    for j in range(nb):
        mrow_ref[j] = jnp.where(j < i, offs[j], jnp.inf)

    last = n_heads - 1
    for h in range(n_heads):
        m_ref[h] = jnp.full((1, blk), SOFTMAX_M_INIT, _F32)
        acc_ref[h] = jnp.zeros((ACC_ROWS, blk), _F32)
    p_ref[last] = jnp.zeros((blk, blk), _BF16)
    alpha_ref[last] = jnp.ones((1, blk), _F32)

    kpos = lax.broadcasted_iota(jnp.int32, (blk, blk), 0)
    qpos = lax.broadcasted_iota(jnp.int32, (blk, blk), 1)
    ones_rows = jnp.ones((ACC_ROWS - HEAD_DIM, blk), _BF16)

    def scores(j, h, own):
        kj = k_ref[0, pl.ds(pl.multiple_of(j * blk, blk), blk), gsl(h)]
        s = jnp.dot(kj, qm_ref[h], preferred_element_type=_F32)
        if own:
            s = jnp.where(kpos <= qpos, s, neg_inf)
        s_ref[h] = s
        cmax_ref[h] = jnp.max(s, axis=0, keepdims=True)

    def expo(j, h, own):
        m_old = m_ref[h]
        if own:
            m_new = jnp.maximum(m_old, cmax_ref[h])
            shift = m_new
        else:
            off = mrow_ref[j, pl.ds(h, 1), :]
            m_new = jnp.maximum(m_old, cmax_ref[h] - off)
            shift = m_new + off
        m_ref[h] = m_new
        alpha_ref[h] = jnp.exp2(m_old - m_new)
        p_ref[h] = jnp.exp2(s_ref[h] - shift).astype(_BF16)

    def pv_unit(j, h):
        vt1 = jnp.concatenate([vt_ref[0, j, h * HEAD_DIM:(h + 1) * HEAD_DIM, :], ones_rows], axis=0)
        pv = jnp.dot(vt1, p_ref[h], preferred_element_type=_F32)
        acc_ref[h] = alpha_ref[h] * acc_ref[h] + pv

    def process_block(j, own, j_next, j_prev):
        for h in range(n_heads):
            ahead = h + SCORE_LOOKAHEAD
            if ahead < n_heads:
                scores(j, ahead, own)
            else:
                scores(j_next, ahead - n_heads, False)
            expo(j, h, own)
            if h == 0:
                pv_unit(j_prev, last)
            else:
                pv_unit(j, h - 1)

    for h in range(SCORE_LOOKAHEAD):
        scores(i, h, True)
    process_block(i, True, 0, i)

    def past_block(j):
        process_block(j, False, jnp.minimum(j + 1, nb - 1), jnp.where(j == 0, i, j - 1))

    def blocks_body(per_trip):
        def body(t, carry):
            for r in range(per_trip):
                past_block(per_trip * t + r)
            return carry
        return body

    n_octs = lax.shift_right_logical(i, 3)
    n_quads = lax.shift_right_logical(i, 2)
    n_pairs = lax.shift_right_logical(i, 1)
    lax.fori_loop(0, n_octs, blocks_body(8), 0)
    lax.fori_loop(2 * n_octs, n_quads, blocks_body(4), 0)
    lax.fori_loop(2 * n_quads, n_pairs, blocks_body(2), 0)
    lax.fori_loop(2 * n_pairs, i, blocks_body(1), 0)
    pv_unit(jnp.where(i == 0, i, i - 1), last)

    ot = jnp.concatenate(
        [acc_ref[h, 0:HEAD_DIM, :] / acc_ref[h, HEAD_DIM:HEAD_DIM + 1, :] for h in range(n_heads)], axis=0)
    o_ref[0, qs, :] = ot.T.astype(_BF16)


def _moba(qt, k, vt):
    bsz, a, s = qt.shape
    nb = s // MOBA_BLOCK
    gw = GROUP_WIDTH * MOBA_GROUPS_PER_STEP
    n_heads = HEADS_PER_GROUP * MOBA_GROUPS_PER_STEP
    assert SCORE_LOOKAHEAD < n_heads and n_heads == SUBLANES
    assert nb % MOBA_Q_BLOCKS_PER_STEP == 0
    q_step = MOBA_Q_BLOCKS_PER_STEP * MOBA_BLOCK
    return pl.pallas_call(
        _moba_kernel,
        grid=(bsz, N_GROUPS // MOBA_GROUPS_PER_STEP, nb // MOBA_Q_BLOCKS_PER_STEP),
        in_specs=[
            pl.BlockSpec((1, gw, q_step), lambda b, g, i: (b, g, i)),
            pl.BlockSpec((1, s, gw), lambda b, g, i: (b, 0, g)),
            pl.BlockSpec((1, nb, gw, MOBA_BLOCK), lambda b, g, i: (b, 0, g, 0)),
        ],
        out_specs=pl.BlockSpec((1, q_step, gw), lambda b, g, i: (b, i, g)),
        out_shape=jax.ShapeDtypeStruct((bsz, s, a), _BF16),
        scratch_shapes=[
            pltpu.VMEM((nb * n_heads, gw), _F32),
            pltpu.VMEM((nb, n_heads, MOBA_BLOCK), _F32),
            pltpu.VMEM((n_heads, GROUP_WIDTH, MOBA_BLOCK), _BF16),
            pltpu.VMEM((n_heads, MOBA_BLOCK, MOBA_BLOCK), _F32),
            pltpu.VMEM((n_heads, MOBA_BLOCK, MOBA_BLOCK), _BF16),
            pltpu.VMEM((n_heads, ACC_ROWS, MOBA_BLOCK), _F32),
            pltpu.VMEM((n_heads, 1, MOBA_BLOCK), _F32),
            pltpu.VMEM((n_heads, 1, MOBA_BLOCK), _F32),
            pltpu.VMEM((n_heads, 1, MOBA_BLOCK), _F32),
        ],
        compiler_params=pltpu.CompilerParams(
            dimension_semantics=("arbitrary", "arbitrary", "arbitrary"), vmem_limit_bytes=VMEM_LIMIT_BYTES),
        name="moba",
    )(qt, k, vt)


def _post_kernel(x_ref, attn_ref, gatt_ref, cnvg_ref, watt_ref, wo_ref, g1_ref, b1_ref,
                 wup_ref, cw_ref, wdown_ref, g2_ref, b2_ref,
                 o_ref, x1_ref, x1b_ref, u_ref, carryg_ref, carryv_ref, hmid_ref):
    tile = x_ref.shape[1]
    half = tile // 2
    halves = (slice(0, half), slice(half, tile))
    n_chunks = D_FF // FF_CHUNK
    gate_cols = lambda c: slice(c * FF_CHUNK, (c + 1) * FF_CHUNK)
    val_cols = lambda c: slice(D_FF + c * FF_CHUNK, D_FF + (c + 1) * FF_CHUNK)

    @pl.when(pl.program_id(1) == 0)
    def _():
        carryg_ref[...] = jnp.zeros_like(carryg_ref)
        carryv_ref[...] = jnp.zeros_like(carryv_ref)

    def attn_out(rs):
        return jnp.dot(attn_ref[0, rs, :], watt_ref[...], preferred_element_type=_F32)

    def gate_mix(rs, y_att):
        t = gatt_ref[0, rs, :].astype(_F32) * y_att + cnvg_ref[0, rs, :].astype(_F32)
        x1b_ref[rs, :] = t.astype(_BF16)

    def mix_proj(rs):
        return jnp.dot(x1b_ref[rs, :], wo_ref[...], preferred_element_type=_F32)

    def norm1(rs, mix):
        x1 = _layer_norm(DEEPNORM_ALPHA * x_ref[0, rs, :] + mix, g1_ref[...], b1_ref[...])
        x1_ref[rs, :] = x1
        x1b_ref[rs, :] = x1.astype(_BF16)

    units = [(rs, c) for rs in halves for c in range(n_chunks)]

    def up_proj(n):
        rs, c = units[n]
        u_ref[n % 2, 0] = jnp.dot(x1b_ref[rs, :], wup_ref[:, gate_cols(c)], preferred_element_type=_F32)
        u_ref[n % 2, 1] = jnp.dot(x1b_ref[rs, :], wup_ref[:, val_cols(c)], preferred_element_type=_F32)

    def conv(u, cols, carry_ref, c):
        out = _causal_conv3(u, carry_ref[c], cw_ref[:, cols])
        carry_ref[c] = u[half - SUBLANES:half]
        return out

    def gated_conv(n):
        rs, c = units[n]
        cg = conv(u_ref[n % 2, 0], gate_cols(c), carryg_ref, c)
        cv = conv(u_ref[n % 2, 1], val_cols(c), carryv_ref, c)
        hmid_ref[c, rs, :] = _silu_times(cg, cv).astype(_BF16)

    def down_proj(rs):
        f = jnp.dot(hmid_ref[0, rs, :], wdown_ref[gate_cols(0), :], preferred_element_type=_F32)
        for c in range(1, n_chunks):
            f = f + jnp.dot(hmid_ref[c, rs, :], wdown_ref[gate_cols(c), :], preferred_element_type=_F32)
        return f

    def norm2(rs, f):
        o_ref[0, rs, :] = _layer_norm(DEEPNORM_ALPHA * x1_ref[rs, :] + f, g2_ref[...], b2_ref[...])

    ha, hb = halves
    ya = attn_out(ha)
    yb = attn_out(hb)
    gate_mix(ha, ya)
    mix_a = mix_proj(ha)
    gate_mix(hb, yb)
    mix_b = mix_proj(hb)
    norm1(ha, mix_a)
    up_proj(0)
    norm1(hb, mix_b)
    for n in range(len(units)):
        if n + 1 < len(units):
            up_proj(n + 1)
        gated_conv(n)
    fa = down_proj(ha)
    fb = down_proj(hb)
    norm2(ha, fa)
    norm2(hb, fb)


def _post(x, attn, gatt, cnvg, watt, wo, g1, b1, wup, cw, wdown, g2, b2):
    bsz, s, d = x.shape
    tile = TOKEN_TILE
    n_chunks = D_FF // FF_CHUNK
    tok = lambda w: pl.BlockSpec((1, tile, w), lambda b, t: (b, t, 0))
    weights = (watt, wo, g1, b1, wup, cw, wdown, g2, b2)
    return pl.pallas_call(
        _post_kernel,
        grid=(bsz, s // tile),
        in_specs=[tok(d), tok(ATTN_WIDTH), tok(d), tok(d)] + [_resident(w.shape) for w in weights],
        out_specs=tok(d),
        out_shape=jax.ShapeDtypeStruct((bsz, s, d), x.dtype),
        scratch_shapes=[
            pltpu.VMEM((tile, d), _F32),
            pltpu.VMEM((tile, d), _BF16),
            pltpu.VMEM((2, 2, tile // 2, FF_CHUNK), _F32),
            pltpu.VMEM((n_chunks, SUBLANES, FF_CHUNK), _F32),
            pltpu.VMEM((n_chunks, SUBLANES, FF_CHUNK), _F32),
            pltpu.VMEM((n_chunks, tile, FF_CHUNK), _BF16),
        ],
        compiler_params=pltpu.CompilerParams(
            dimension_semantics=("arbitrary", "arbitrary"), vmem_limit_bytes=VMEM_LIMIT_BYTES),
        name="post",
    )(x, attn, gatt, cnvg, *weights)


def kernel(x, positions, w_in, b_gate, w_attn_out, conv_w_mix, w_conv_out, w_o,
           ln1_g, ln1_b, w_up, conv_w_ffn, w_down, ln2_g, ln2_b):
    bsz, s, d = x.shape
    assert s % TOKEN_TILE == 0 and TOKEN_TILE % MOBA_BLOCK == 0 and D_FF % FF_CHUNK == 0
    a = ATTN_WIDTH
    inv_freq = (ROPE_THETA ** (-jnp.arange(0, ROT_DIM, 2, dtype=_F32) / ROT_DIM)).reshape(ROT_HALF, 1)
    pos3 = positions.reshape(bsz, 1, s)
    for l in range(DEPTH):
        wqkvt = w_in[l, :, :3 * a].T.astype(_BF16)
        wrest = w_in[l, :, 3 * a:].astype(_BF16)
        qt, k, vt, gatt, cnvg = _inproj(
            x, pos3, inv_freq, wqkvt, wrest, b_gate[l].reshape(1, -1), conv_w_mix[l],
            w_conv_out[l].astype(_BF16))
        attn = _moba(qt, k, vt)
        row = lambda v: v.reshape(1, -1)
        x = _post(
            x, attn, gatt, cnvg, w_attn_out[l].astype(_BF16), w_o[l].astype(_BF16),
            row(ln1_g[l]), row(ln1_b[l]),
            w_up[l].astype(_BF16), conv_w_ffn[l], w_down[l].astype(_BF16),
            row(ln2_g[l]), row(ln2_b[l]))
    return x
```

```python
import math

import jax
import jax.numpy as jnp
from jax import lax
from jax.experimental import pallas as pl
from jax.experimental.pallas import tpu as pltpu

N_HEADS = 8
HEAD_DIM = 64
ATTN_WIDTH = N_HEADS * HEAD_DIM
MOBA_BLOCK = 256
MOBA_TOPK = 3
ROPE_THETA = 500000.0
ROT_DIM = HEAD_DIM // 4
ROT_HALF = ROT_DIM // 2
CONV_WIDTH = 512
CONV_K = 3
D_FF = 2816
DEPTH = 1
DEEPNORM_ALPHA = (2.0 * DEPTH) ** 0.25
LN_EPS = 1e-5

Q_SCALE = (1.0 / math.sqrt(HEAD_DIM)) * math.log2(math.e)

HEADS_PER_GROUP = 2
GROUP_WIDTH = HEADS_PER_GROUP * HEAD_DIM
N_GROUPS = N_HEADS // HEADS_PER_GROUP
MOBA_GROUPS_PER_STEP = 4
MOBA_Q_BLOCKS_PER_STEP = 2
SCORE_LOOKAHEAD = 4
SOFTMAX_M_INIT = -1e30
BF16_SUBLANES = 16
ACC_ROWS = HEAD_DIM + BF16_SUBLANES
SUBLANES = 8
TOKEN_TILE = 512
FF_CHUNK = 256
VMEM_LIMIT_BYTES = 56 * 1024 * 1024

_NT_DIMS = (((1,), (1,)), ((), ()))
_F32 = jnp.float32
_BF16 = jnp.bfloat16


def _resident(shape):
    nd = len(shape)
    return pl.BlockSpec(shape, lambda *_: (0,) * nd, pipeline_mode=pl.Buffered(1))


def _causal_conv3(cur, carry, w):
    rows, width = cur.shape
    groups = rows // SUBLANES
    is_last = lax.broadcasted_iota(jnp.int32, (groups, SUBLANES, width), 1) == SUBLANES - 1

    def shift1(a, a_before):
        a3 = a.reshape(groups, SUBLANES, width)
        prev3 = jnp.concatenate([a_before[None], a3[:-1]], axis=0)
        return pltpu.roll(jnp.where(is_last, prev3, a3), 1, axis=1).reshape(rows, width)

    inner = w[1:2] * cur + shift1(w[0:1] * cur, w[0:1] * carry)
    inner_before = w[1:2] * carry + w[0:1] * pltpu.roll(carry, 1, axis=0)
    return w[2:3] * cur + shift1(inner, inner_before)


def _tree_reduce(fn, xs):
    while len(xs) > 1:
        xs = [fn(xs[k], xs[k + 1]) for k in range(0, len(xs) - 1, 2)] + (xs[-1:] if len(xs) % 2 else [])
    return xs[0]


def _sigmoid(x):
    return 0.5 * jnp.tanh(0.5 * x) + 0.5


def _silu_times(g, v):
    h = 0.5 * g
    return h * (jnp.tanh(h) + 1.0) * v


def _layer_norm(z, g, b):
    mu = jnp.mean(z, axis=-1, keepdims=True)
    zc = z - mu
    var = jnp.mean(zc * zc, axis=-1, keepdims=True)
    return zc * lax.rsqrt(var + LN_EPS) * g + b


def _inproj_kernel(x_ref, pos_ref, invf_ref, wqkvt_ref, wrest_ref, bg_ref, cw_ref, wcnv_ref,
                   qt_ref, k_ref, vt_ref, gatt_ref, cnvg_ref, carry_ref):
    tile = x_ref.shape[1]
    d_model = x_ref.shape[2]
    c = CONV_WIDTH
    xb = x_ref[0].astype(_BF16)

    gates_pre = jnp.dot(xb, wrest_ref[:, 3 * c:], preferred_element_type=_F32)
    qkvt = lax.dot_general(wqkvt_ref[...], xb, _NT_DIMS, preferred_element_type=_F32)
    gates = _sigmoid(gates_pre + bg_ref[...])
    gatt_ref[0] = gates[:, :d_model].astype(_BF16)
    g_cnv = gates[:, d_model:]

    rest = jnp.dot(xb, wrest_ref[:, 0:3 * c], preferred_element_type=_F32)
    ang = invf_ref[...] * pos_ref[0].astype(_F32)
    cos, sin = jnp.cos(ang), jnp.sin(ang)

    def rope_t(t):
        parts = []
        for h in range(N_HEADS):
            r = h * HEAD_DIM
            x1 = t[r:r + ROT_HALF]
            x2 = t[r + ROT_HALF:r + ROT_DIM]
            parts += [x1 * cos - x2 * sin, x2 * cos + x1 * sin, t[r + ROT_DIM:r + HEAD_DIM]]
        return jnp.concatenate(parts, axis=0)

    a = ATTN_WIDTH
    qt_ref[0] = (rope_t(qkvt[0:a]) * Q_SCALE).astype(_BF16)
    k_ref[0] = rope_t(qkvt[a:2 * a]).T.astype(_BF16)
    vt = qkvt[2 * a:3 * a].astype(_BF16)
    for j in range(tile // MOBA_BLOCK):
        vt_ref[0, j] = vt[:, j * MOBA_BLOCK:(j + 1) * MOBA_BLOCK]

    gate_b, gate_c, h_in = rest[:, 0:c], rest[:, c:2 * c], rest[:, 2 * c:3 * c]
    ch = gate_c * h_in

    @pl.when(pl.program_id(1) == 0)
    def _():
        carry_ref[...] = jnp.zeros_like(carry_ref)

    conv = _causal_conv3(ch, carry_ref[...], cw_ref[...])
    carry_ref[...] = ch[tile - SUBLANES:tile]
    y_cnv = jnp.dot((gate_b * conv).astype(_BF16), wcnv_ref[...], preferred_element_type=_F32)
    cnvg_ref[0] = (g_cnv * y_cnv).astype(_BF16)


def _inproj(x, pos3, inv_freq, wqkvt, wrest, bg, cw, wcnv):
    bsz, s, d = x.shape
    tile = TOKEN_TILE
    nb = s // MOBA_BLOCK
    a = ATTN_WIDTH
    tok = lambda w: pl.BlockSpec((1, tile, w), lambda b, t: (b, t, 0))
    return pl.pallas_call(
        _inproj_kernel,
        grid=(bsz, s // tile),
        in_specs=[
            tok(d),
            pl.BlockSpec((1, 1, tile), lambda b, t: (b, 0, t)),
            _resident(inv_freq.shape),
            _resident(wqkvt.shape),
            _resident(wrest.shape),
            _resident(bg.shape),
            _resident(cw.shape),
            _resident(wcnv.shape),
        ],
        out_specs=[
            pl.BlockSpec((1, a, tile), lambda b, t: (b, 0, t)),
            tok(a),
            pl.BlockSpec((1, tile // MOBA_BLOCK, a, MOBA_BLOCK), lambda b, t: (b, t, 0, 0)),
            tok(d),
            tok(d),
        ],
        out_shape=[
            jax.ShapeDtypeStruct((bsz, a, s), _BF16),
            jax.ShapeDtypeStruct((bsz, s, a), _BF16),
            jax.ShapeDtypeStruct((bsz, nb, a, MOBA_BLOCK), _BF16),
            jax.ShapeDtypeStruct((bsz, s, d), _BF16),
            jax.ShapeDtypeStruct((bsz, s, d), _BF16),
        ],
        scratch_shapes=[pltpu.VMEM((SUBLANES, CONV_WIDTH), _F32)],
        compiler_params=pltpu.CompilerParams(
            dimension_semantics=("arbitrary", "arbitrary"), vmem_limit_bytes=VMEM_LIMIT_BYTES),
        name="inproj",
    )(x, pos3, inv_freq, wqkvt, wrest, bg, cw, wcnv)


def _moba_kernel(qt_ref, *refs):
    step = pl.program_id(2)
    for r in range(MOBA_Q_BLOCKS_PER_STEP):
        first_of_batch = (step == 0) if r == 0 else None
        _moba_query_block(step * MOBA_Q_BLOCKS_PER_STEP + r, slice(r * MOBA_BLOCK, (r + 1) * MOBA_BLOCK),
                          first_of_batch, qt_ref, *refs)


def _moba_query_block(i, qs, first_of_batch, qt_ref, k_ref, vt_ref, o_ref, kmean_ref, mrow_ref, qm_ref, s_ref,
                      p_ref, acc_ref, m_ref, cmax_ref, alpha_ref):
    nb = vt_ref.shape[1]
    blk = MOBA_BLOCK
    neg_inf = -jnp.inf

    n_heads = qt_ref.shape[1] // HEAD_DIM
    width = qt_ref.shape[1]
    gw = GROUP_WIDTH

    def block_means():
        chan = lax.broadcasted_iota(jnp.int32, (n_heads, width), 1)
        head0 = lax.broadcasted_iota(jnp.int32, (n_heads, width), 0) * HEAD_DIM
        in_head = (chan >= head0) & (chan < head0 + HEAD_DIM)

        def mean_body(j, _):
            kb = k_ref[0, pl.ds(pl.multiple_of(j * blk, blk), blk), :].astype(_F32)
            km = jnp.sum(kb, axis=0, keepdims=True) / blk
            kmean_ref[pl.ds(pl.multiple_of(j * n_heads, n_heads), n_heads), :] = jnp.where(in_head, km, 0.0)
            return 0
        lax.fori_loop(0, nb, mean_body, 0)

    if first_of_batch is not None:
        pl.when(first_of_batch)(block_means)

    grow = lax.broadcasted_iota(jnp.int32, (gw, blk), 0)
    for h in range(n_heads):
        g, hh = divmod(h, HEADS_PER_GROUP)
        qg = qt_ref[0, g * gw:(g + 1) * gw, qs]
        qm_ref[h] = jnp.where((grow >= hh * HEAD_DIM) & (grow < (hh + 1) * HEAD_DIM), qg, jnp.zeros_like(qg))
    gsl = lambda h: slice((h // HEADS_PER_GROUP) * gw, (h // HEADS_PER_GROUP + 1) * gw)

    sc_all = jnp.dot(kmean_ref[...].astype(_BF16), qt_ref[0, :, qs], preferred_element_type=_F32)
    scs = [jnp.where(j < i, sc_all[j * n_heads:(j + 1) * n_heads], neg_inf) for j in range(nb)]
    offs = [jnp.full((n_heads, blk), jnp.inf, _F32)] * nb
    for _ in range(MOBA_TOPK):
        best = _tree_reduce(jnp.maximum, scs)
        idx = _tree_reduce(jnp.minimum, [jnp.where(scs[j] == best, j, nb) for j in range(nb)])
        picks = [idx == j for j in range(nb)]
        offs = [jnp.where(picks[j], 0.0, offs[j]) for j in range(nb)]
        scs = [jnp.where(picks[j], neg_inf, scs[j]) for j in range(nb)]
    for j in range(nb):
        mrow_ref[j] = jnp.where(j < i, offs[j], jnp.inf)

    last = n_heads - 1
    for h in range(n_heads):
        m_ref[h] = jnp.full((1, blk), SOFTMAX_M_INIT, _F32)
        acc_ref[h] = jnp.zeros((ACC_ROWS, blk), _F32)
    p_ref[last] = jnp.zeros((blk, blk), _BF16)
    alpha_ref[last] = jnp.ones((1, blk), _F32)

    kpos = lax.broadcasted_iota(jnp.int32, (blk, blk), 0)
    qpos = lax.broadcasted_iota(jnp.int32, (blk, blk), 1)
    ones_rows = jnp.ones((ACC_ROWS - HEAD_DIM, blk), _BF16)

    def scores(j, h, own):
        kj = k_ref[0, pl.ds(pl.multiple_of(j * blk, blk), blk), gsl(h)]
        s = jnp.dot(kj, qm_ref[h], preferred_element_type=_F32)
        if own:
            s = jnp.where(kpos <= qpos, s, neg_inf)
        s_ref[h] = s
        cmax_ref[h] = jnp.max(s, axis=0, keepdims=True)

    def expo(j, h, own):
        m_old = m_ref[h]
        if own:
            m_new = jnp.maximum(m_old, cmax_ref[h])
            shift = m_new
        else:
            off = mrow_ref[j, pl.ds(h, 1), :]
            m_new = jnp.maximum(m_old, cmax_ref[h] - off)
            shift = m_new + off
        m_ref[h] = m_new
        alpha_ref[h] = jnp.exp2(m_old - m_new)
        p_ref[h] = jnp.exp2(s_ref[h] - shift).astype(_BF16)

    def pv_unit(j, h):
        vt1 = jnp.concatenate([vt_ref[0, j, h * HEAD_DIM:(h + 1) * HEAD_DIM, :], ones_rows], axis=0)
        pv = jnp.dot(vt1, p_ref[h], preferred_element_type=_F32)
        acc_ref[h] = alpha_ref[h] * acc_ref[h] + pv

    def process_block(j, own, j_next, j_prev):
        for h in range(n_heads):
            ahead = h + SCORE_LOOKAHEAD
            if ahead < n_heads:
                scores(j, ahead, own)
            else:
                scores(j_next, ahead - n_heads, False)
            expo(j, h, own)
            if h == 0:
                pv_unit(j_prev, last)
            else:
                pv_unit(j, h - 1)

    for h in range(SCORE_LOOKAHEAD):
        scores(i, h, True)
    process_block(i, True, 0, i)

    def past_block(j):
        process_block(j, False, jnp.minimum(j + 1, nb - 1), jnp.where(j == 0, i, j - 1))

    def blocks_body(per_trip):
        def body(t, carry):
            for r in range(per_trip):
                past_block(per_trip * t + r)
            return carry
        return body

    n_octs = lax.shift_right_logical(i, 3)
    n_quads = lax.shift_right_logical(i, 2)
    n_pairs = lax.shift_right_logical(i, 1)
    lax.fori_loop(0, n_octs, blocks_body(8), 0)
    lax.fori_loop(2 * n_octs, n_quads, blocks_body(4), 0)
    lax.fori_loop(2 * n_quads, n_pairs, blocks_body(2), 0)
    lax.fori_loop(2 * n_pairs, i, blocks_body(1), 0)
    pv_unit(jnp.where(i == 0, i, i - 1), last)

    ot = jnp.concatenate(
        [acc_ref[h, 0:HEAD_DIM, :] / acc_ref[h, HEAD_DIM:HEAD_DIM + 1, :] for h in range(n_heads)], axis=0)
    o_ref[0, qs, :] = ot.T.astype(_BF16)


def _moba(qt, k, vt):
    bsz, a, s = qt.shape
    nb = s // MOBA_BLOCK
    gw = GROUP_WIDTH * MOBA_GROUPS_PER_STEP
    n_heads = HEADS_PER_GROUP * MOBA_GROUPS_PER_STEP
    assert SCORE_LOOKAHEAD < n_heads and n_heads == SUBLANES
    assert nb % MOBA_Q_BLOCKS_PER_STEP == 0
    q_step = MOBA_Q_BLOCKS_PER_STEP * MOBA_BLOCK
    return pl.pallas_call(
        _moba_kernel,
        grid=(bsz, N_GROUPS // MOBA_GROUPS_PER_STEP, nb // MOBA_Q_BLOCKS_PER_STEP),
        in_specs=[
            pl.BlockSpec((1, gw, q_step), lambda b, g, i: (b, g, i)),
            pl.BlockSpec((1, s, gw), lambda b, g, i: (b, 0, g)),
            pl.BlockSpec((1, nb, gw, MOBA_BLOCK), lambda b, g, i: (b, 0, g, 0)),
        ],
        out_specs=pl.BlockSpec((1, q_step, gw), lambda b, g, i: (b, i, g)),
        out_shape=jax.ShapeDtypeStruct((bsz, s, a), _BF16),
        scratch_shapes=[
            pltpu.VMEM((nb * n_heads, gw), _F32),
            pltpu.VMEM((nb, n_heads, MOBA_BLOCK), _F32),
            pltpu.VMEM((n_heads, GROUP_WIDTH, MOBA_BLOCK), _BF16),
            pltpu.VMEM((n_heads, MOBA_BLOCK, MOBA_BLOCK), _F32),
            pltpu.VMEM((n_heads, MOBA_BLOCK, MOBA_BLOCK), _BF16),
            pltpu.VMEM((n_heads, ACC_ROWS, MOBA_BLOCK), _F32),
            pltpu.VMEM((n_heads, 1, MOBA_BLOCK), _F32),
            pltpu.VMEM((n_heads, 1, MOBA_BLOCK), _F32),
            pltpu.VMEM((n_heads, 1, MOBA_BLOCK), _F32),
        ],
        compiler_params=pltpu.CompilerParams(
            dimension_semantics=("arbitrary", "arbitrary", "arbitrary"), vmem_limit_bytes=VMEM_LIMIT_BYTES),
        name="moba",
    )(qt, k, vt)


def _post_kernel(x_ref, attn_ref, gatt_ref, cnvg_ref, watt_ref, wo_ref, g1_ref, b1_ref,
                 wup_ref, cw_ref, wdown_ref, g2_ref, b2_ref,
                 o_ref, x1_ref, x1b_ref, u_ref, carryg_ref, carryv_ref, hmid_ref):
    tile = x_ref.shape[1]
    half = tile // 2
    halves = (slice(0, half), slice(half, tile))
    n_chunks = D_FF // FF_CHUNK
    gate_cols = lambda c: slice(c * FF_CHUNK, (c + 1) * FF_CHUNK)
    val_cols = lambda c: slice(D_FF + c * FF_CHUNK, D_FF + (c + 1) * FF_CHUNK)

    @pl.when(pl.program_id(1) == 0)
    def _():
        carryg_ref[...] = jnp.zeros_like(carryg_ref)
        carryv_ref[...] = jnp.zeros_like(carryv_ref)

    def attn_out(rs):
        return jnp.dot(attn_ref[0, rs, :], watt_ref[...], preferred_element_type=_F32)

    def gate_mix(rs, y_att):
        t = gatt_ref[0, rs, :].astype(_F32) * y_att + cnvg_ref[0, rs, :].astype(_F32)
        x1b_ref[rs, :] = t.astype(_BF16)

    def mix_proj(rs):
        return jnp.dot(x1b_ref[rs, :], wo_ref[...], preferred_element_type=_F32)

    def norm1(rs, mix):
        x1 = _layer_norm(DEEPNORM_ALPHA * x_ref[0, rs, :] + mix, g1_ref[...], b1_ref[...])
        x1_ref[rs, :] = x1
        x1b_ref[rs, :] = x1.astype(_BF16)

    units = [(rs, c) for rs in halves for c in range(n_chunks)]

    def up_proj(n):
        rs, c = units[n]
        u_ref[n % 2, 0] = jnp.dot(x1b_ref[rs, :], wup_ref[:, gate_cols(c)], preferred_element_type=_F32)
        u_ref[n % 2, 1] = jnp.dot(x1b_ref[rs, :], wup_ref[:, val_cols(c)], preferred_element_type=_F32)

    def conv(u, cols, carry_ref, c):
        out = _causal_conv3(u, carry_ref[c], cw_ref[:, cols])
        carry_ref[c] = u[half - SUBLANES:half]
        return out

    def gated_conv(n):
        rs, c = units[n]
        cg = conv(u_ref[n % 2, 0], gate_cols(c), carryg_ref, c)
        cv = conv(u_ref[n % 2, 1], val_cols(c), carryv_ref, c)
        hmid_ref[c, rs, :] = _silu_times(cg, cv).astype(_BF16)

    def down_proj(rs):
        f = jnp.dot(hmid_ref[0, rs, :], wdown_ref[gate_cols(0), :], preferred_element_type=_F32)
        for c in range(1, n_chunks):
            f = f + jnp.dot(hmid_ref[c, rs, :], wdown_ref[gate_cols(c), :], preferred_element_type=_F32)
        return f

    def norm2(rs, f):
        o_ref[0, rs, :] = _layer_norm(DEEPNORM_ALPHA * x1_ref[rs, :] + f, g2_ref[...], b2_ref[...])

    ha, hb = halves
    ya = attn_out(ha)
    yb = attn_out(hb)
    gate_mix(ha, ya)
    mix_a = mix_proj(ha)
    gate_mix(hb, yb)
    mix_b = mix_proj(hb)
    norm1(ha, mix_a)
    up_proj(0)
    norm1(hb, mix_b)
    for n in range(len(units)):
        if n + 1 < len(units):
            up_proj(n + 1)
        gated_conv(n)
    fa = down_proj(ha)
    fb = down_proj(hb)
    norm2(ha, fa)
    norm2(hb, fb)


def _post(x, attn, gatt, cnvg, watt, wo, g1, b1, wup, cw, wdown, g2, b2):
    bsz, s, d = x.shape
    tile = TOKEN_TILE
    n_chunks = D_FF // FF_CHUNK
    tok = lambda w: pl.BlockSpec((1, tile, w), lambda b, t: (b, t, 0))
    weights = (watt, wo, g1, b1, wup, cw, wdown, g2, b2)
    return pl.pallas_call(
        _post_kernel,
        grid=(bsz, s // tile),
        in_specs=[tok(d), tok(ATTN_WIDTH), tok(d), tok(d)] + [_resident(w.shape) for w in weights],
        out_specs=tok(d),
        out_shape=jax.ShapeDtypeStruct((bsz, s, d), x.dtype),
        scratch_shapes=[
            pltpu.VMEM((tile, d), _F32),
            pltpu.VMEM((tile, d), _BF16),
            pltpu.VMEM((2, 2, tile // 2, FF_CHUNK), _F32),
            pltpu.VMEM((n_chunks, SUBLANES, FF_CHUNK), _F32),
            pltpu.VMEM((n_chunks, SUBLANES, FF_CHUNK), _F32),
            pltpu.VMEM((n_chunks, tile, FF_CHUNK), _BF16),
        ],
        compiler_params=pltpu.CompilerParams(
            dimension_semantics=("arbitrary", "arbitrary"), vmem_limit_bytes=VMEM_LIMIT_BYTES),
        name="post",
    )(x, attn, gatt, cnvg, *weights)


def kernel(x, positions, w_in, b_gate, w_attn_out, conv_w_mix, w_conv_out, w_o,
           ln1_g, ln1_b, w_up, conv_w_ffn, w_down, ln2_g, ln2_b):
    bsz, s, d = x.shape
    assert s % TOKEN_TILE == 0 and TOKEN_TILE % MOBA_BLOCK == 0 and D_FF % FF_CHUNK == 0
    a = ATTN_WIDTH
    inv_freq = (ROPE_THETA ** (-jnp.arange(0, ROT_DIM, 2, dtype=_F32) / ROT_DIM)).reshape(ROT_HALF, 1)
    pos3 = positions.reshape(bsz, 1, s)
    for l in range(DEPTH):
        wqkvt = w_in[l, :, :3 * a].T.astype(_BF16)
        wrest = w_in[l, :, 3 * a:].astype(_BF16)
        qt, k, vt, gatt, cnvg = _inproj(
            x, pos3, inv_freq, wqkvt, wrest, b_gate[l].reshape(1, -1), conv_w_mix[l],
            w_conv_out[l].astype(_BF16))
        attn = _moba(qt, k, vt)
        row = lambda v: v.reshape(1, -1)
        x = _post(
            x, attn, gatt, cnvg, w_attn_out[l].astype(_BF16), w_o[l].astype(_BF16),
            row(ln1_g[l]), row(ln1_b[l]),
            w_up[l].astype(_BF16), conv_w_ffn[l], w_down[l].astype(_BF16),
            row(ln2_g[l]), row(ln2_b[l]))
    return x
```
